```python
import jax, jax.numpy as jnp
from jax import lax
import numpy as np

D_MODEL = 1024
BATCH = 32
SEQ = 2048
DEPTH = 2

CHUNK = 64
Q_BLOCK = 128
N_A_LAYERS = max(1, DEPTH // 2)
N_B_LAYERS = DEPTH - N_A_LAYERS
N_DENSE = (DEPTH + 1) // 2
N_MOE = DEPTH // 2

MLA_HEADS = 16
MLA_NOPE = 64
MLA_ROPE = 32
MLA_QK = MLA_NOPE + MLA_ROPE
MLA_V = 64
Q_LORA = 384
KV_LORA = 256
ROPE_THETA = 10000.0

CA_HEADS = 16
CA_HEAD_DIM = D_MODEL // CA_HEADS
LEFT_CHUNKS = 8
BAND = (LEFT_CHUNKS + 1) * CHUNK
REL_CLIP = 128

DENSE_FF = 2816
N_EXPERTS = 8
TOP_K = 2
EXPERT_FF = 3584

EPS = 1e-6

kernel_name = "yoco_mla_chunkband_moe_trunk"


def rms_norm(x, g):
    xf = x.astype(jnp.float32)
    y = xf * lax.rsqrt(jnp.mean(xf * xf, axis=-1, keepdims=True) + EPS)
    return (y * g.astype(jnp.float32)).astype(x.dtype)


def rope(x, pos):
    half = x.shape[-1] // 2
    inv = ROPE_THETA ** (-jnp.arange(half, dtype=jnp.float32) / half)
    ang = pos.astype(jnp.float32)[:, None] * inv[None, :]
    cos = jnp.cos(ang)[:, None, :]
    sin = jnp.sin(ang)[:, None, :]
    xf = x.astype(jnp.float32)
    x1, x2 = xf[..., :half], xf[..., half:]
    return jnp.concatenate([x1 * cos - x2 * sin, x2 * cos + x1 * sin], axis=-1).astype(x.dtype)


def swiglu(h, w1, w3, w2):
    return (jax.nn.silu(h @ w1) * (h @ w3)) @ w2


def block_causal_attention(q, k, v, scale):
    B, S, H, Dq = q.shape
    Dv = v.shape[-1]
    nb = S // Q_BLOCK
    qb = q.reshape(B, nb, Q_BLOCK, H, Dq).transpose(1, 0, 2, 3, 4)
    kchunk = jnp.arange(S) // CHUNK
    neg = jnp.finfo(jnp.float32).min

    def one(args):
        qblk, ib = args
        s = jnp.einsum('bqhd,bkhd->bhqk', qblk, k).astype(jnp.float32) * scale
        qchunk = (ib * Q_BLOCK + jnp.arange(Q_BLOCK)) // CHUNK
        mask = kchunk[None, :] <= qchunk[:, None]
        s = jnp.where(mask[None, None], s, neg)
        p = jax.nn.softmax(s, axis=-1).astype(v.dtype)
        return jnp.einsum('bhqk,bkhd->bqhd', p, v)

    o = lax.map(one, (qb, jnp.arange(nb)))
    return o.transpose(1, 0, 2, 3, 4).reshape(B, S, H, Dv)


def mla_attention(h, w_dq, g_cq, w_uq, w_dkv, g_ckv, w_ukv, g_qn, g_kn, w_o):
    B, S, _ = h.shape
    pos = jnp.arange(S)
    c_q = rms_norm(h @ w_dq, g_cq)
    q = (c_q @ w_uq).reshape(B, S, MLA_HEADS, MLA_QK)
    kv_a = h @ w_dkv
    c_kv = rms_norm(kv_a[..., :KV_LORA], g_ckv)
    k_pe = jnp.broadcast_to(kv_a[:, :, None, KV_LORA:], (B, S, MLA_HEADS, MLA_ROPE))
    kv = (c_kv @ w_ukv).reshape(B, S, MLA_HEADS, MLA_NOPE + MLA_V)
    k_nope, v = kv[..., :MLA_NOPE], kv[..., MLA_NOPE:]
    k = jnp.concatenate([k_nope, k_pe], axis=-1)
    q = rms_norm(q, g_qn)
    k = rms_norm(k, g_kn)
    q = jnp.concatenate([q[..., :MLA_NOPE], rope(q[..., MLA_NOPE:], pos)], axis=-1)
    k = jnp.concatenate([k[..., :MLA_NOPE], rope(k[..., MLA_NOPE:], pos)], axis=-1)
    o = block_causal_attention(q, k, v, MLA_QK ** -0.5)
    return o.reshape(B, S, MLA_HEADS * MLA_V) @ w_o


def shared_band_kv(x, g_kv, w_k, w_v, g_kn):
    B, S, _ = x.shape
    h = rms_norm(x, g_kv)
    k = rms_norm((h @ w_k).reshape(B, S, CA_HEADS, CA_HEAD_DIM), g_kn)
    v = (h @ w_v).reshape(B, S, CA_HEADS, CA_HEAD_DIM)
    pad = ((0, 0), (LEFT_CHUNKS * CHUNK, 0), (0, 0), (0, 0))
    return jnp.pad(k, pad), jnp.pad(v, pad)


def chunked_band_attention(h, w_q, g_qn, rel_bias, w_o, k_pad, v_pad):
    B, S, _ = h.shape
    nc = S // CHUNK
    q = rms_norm((h @ w_q).reshape(B, S, CA_HEADS, CA_HEAD_DIM), g_qn)
    i = jnp.arange(CHUNK)
    j = jnp.arange(BAND)
    rel = i[:, None] + LEFT_CHUNKS * CHUNK - j[None, :]
    idx = jnp.clip(rel, -REL_CLIP, REL_CLIP) + REL_CLIP
    bias = rel_bias[:, idx].astype(jnp.float32)
    qc = q.reshape(B, nc, CHUNK, CA_HEADS, CA_HEAD_DIM).transpose(1, 0, 2, 3, 4)
    neg = jnp.finfo(jnp.float32).min
    scale = CA_HEAD_DIM ** -0.5

    def one(args):
        qch, c = args
        kb = lax.dynamic_slice_in_dim(k_pad, c * CHUNK, BAND, axis=1)
        vb = lax.dynamic_slice_in_dim(v_pad, c * CHUNK, BAND, axis=1)
        s = jnp.einsum('bqhd,bkhd->bhqk', qch, kb).astype(jnp.float32) * scale + bias[None]
        valid = (c - LEFT_CHUNKS) * CHUNK + j >= 0
        s = jnp.where(valid[None, None, None, :], s, neg)
        p = jax.nn.softmax(s, axis=-1).astype(vb.dtype)
        return jnp.einsum('bhqk,bkhd->bqhd', p, vb)

    o = lax.map(one, (qc, jnp.arange(nc)))
    o = o.transpose(1, 0, 2, 3, 4).reshape(B, S, CA_HEADS * CA_HEAD_DIM)
    return o @ w_o


def moe_swiglu(h, w_router, w1, w3, w2):
    B, S, D = h.shape
    t = h.reshape(B * S, D)
    logits = (t @ w_router).astype(jnp.float32)
    top_vals, top_idx = lax.top_k(logits, TOP_K)
    gates = jax.nn.softmax(top_vals, axis=-1)
    combine = jnp.sum(jax.nn.one_hot(top_idx, N_EXPERTS, dtype=jnp.float32) * gates[..., None], axis=1)
    y = jnp.zeros_like(t)
    for e in range(N_EXPERTS):
        y = y + combine[:, e:e + 1].astype(t.dtype) * swiglu(t, w1[e], w3[e], w2[e])
    return y.reshape(B, S, D)


def setup_inputs(seed: int = 0) -> dict:
    key = jax.random.key(seed)
    ks = list(jax.random.split(key, 32))
    f32 = jnp.float32

    def w(k, shape, fan_in):
        return jax.random.normal(k, shape, f32) * (fan_in ** -0.5)

    def gain(k, shape):
        return 1.0 + 0.02 * jax.random.normal(k, shape, f32)

    D = D_MODEL
    return {
        "x": jax.random.normal(ks[0], (BATCH, SEQ, D), f32),
        "attn_norm": gain(ks[1], (DEPTH, D)),
        "ffn_norm": gain(ks[2], (DEPTH, D)),
        "a_w_dq": w(ks[3], (N_A_LAYERS, D, Q_LORA), D),
        "a_g_cq": gain(ks[4], (N_A_LAYERS, Q_LORA)),
        "a_w_uq": w(ks[5], (N_A_LAYERS, Q_LORA, MLA_HEADS * MLA_QK), Q_LORA),
        "a_w_dkv": w(ks[6], (N_A_LAYERS, D, KV_LORA + MLA_ROPE), D),
        "a_g_ckv": gain(ks[7], (N_A_LAYERS, KV_LORA)),
        "a_w_ukv": w(ks[8], (N_A_LAYERS, KV_LORA, MLA_HEADS * (MLA_NOPE + MLA_V)), KV_LORA),
        "a_g_qn": gain(ks[9], (N_A_LAYERS, MLA_QK)),
        "a_g_kn": gain(ks[10], (N_A_LAYERS, MLA_QK)),
        "a_w_o": w(ks[11], (N_A_LAYERS, MLA_HEADS * MLA_V, D), MLA_HEADS * MLA_V),
        "kv_norm": gain(ks[12], (D,)),
        "kv_w_k": w(ks[13], (D, CA_HEADS * CA_HEAD_DIM), D),
        "kv_w_v": w(ks[14], (D, CA_HEADS * CA_HEAD_DIM), D),
        "kv_g_kn": gain(ks[15], (CA_HEAD_DIM,)),
        "b_w_q": w(ks[16], (N_B_LAYERS, D, CA_HEADS * CA_HEAD_DIM), D),
        "b_g_qn": gain(ks[17], (N_B_LAYERS, CA_HEAD_DIM)),
        "b_rel_bias": 0.1 * jax.random.normal(ks[18], (N_B_LAYERS, CA_HEADS, 2 * REL_CLIP + 1), f32),
        "b_w_o": w(ks[19], (N_B_LAYERS, CA_HEADS * CA_HEAD_DIM, D), CA_HEADS * CA_HEAD_DIM),
        "ffn_w1": w(ks[20], (N_DENSE, D, DENSE_FF), D),
        "ffn_w3": w(ks[21], (N_DENSE, D, DENSE_FF), D),
        "ffn_w2": w(ks[22], (N_DENSE, DENSE_FF, D), DENSE_FF),
        "moe_router": w(ks[23], (N_MOE, D, N_EXPERTS), D),
        "moe_w1": w(ks[24], (N_MOE, N_EXPERTS, D, EXPERT_FF), D),
        "moe_w3": w(ks[25], (N_MOE, N_EXPERTS, D, EXPERT_FF), D),
        "moe_w2": w(ks[26], (N_MOE, N_EXPERTS, EXPERT_FF, D), EXPERT_FF),
    }


def reference(x, attn_norm, ffn_norm, a_w_dq, a_g_cq, a_w_uq, a_w_dkv, a_g_ckv, a_w_ukv,
              a_g_qn, a_g_kn, a_w_o, kv_norm, kv_w_k, kv_w_v, kv_g_kn, b_w_q, b_g_qn,
              b_rel_bias, b_w_o, ffn_w1, ffn_w3, ffn_w2, moe_router, moe_w1, moe_w3, moe_w2):
    k_pad = None
    v_pad = None
    for layer in range(DEPTH):
        h = rms_norm(x, attn_norm[layer])
        if layer < N_A_LAYERS:
            la = layer
            x = x + mla_attention(h, a_w_dq[la], a_g_cq[la], a_w_uq[la], a_w_dkv[la], a_g_ckv[la],
                                  a_w_ukv[la], a_g_qn[la], a_g_kn[la], a_w_o[la])
        else:
            lb = layer - N_A_LAYERS
            x = x + chunked_band_attention(h, b_w_q[lb], b_g_qn[lb], b_rel_bias[lb], b_w_o[lb],
                                           k_pad, v_pad)
        h = rms_norm(x, ffn_norm[layer])
        if layer % 2 == 0:
            ld = layer // 2
            x = x + swiglu(h, ffn_w1[ld], ffn_w3[ld], ffn_w2[ld])
        else:
            lm = layer // 2
            x = x + moe_swiglu(h, moe_router[lm], moe_w1[lm], moe_w3[lm], moe_w2[lm])
        if layer == N_A_LAYERS - 1:
            k_pad, v_pad = shared_band_kv(x, kv_norm, kv_w_k, kv_w_v, kv_g_kn)
    return x
```

```python
import functools

import jax
import jax.numpy as jnp
import numpy as np
from jax import lax
from jax.experimental import pallas as pl
from jax.experimental.pallas import tpu as pltpu

F32 = jnp.float32
BF16 = jnp.bfloat16

EPS = 1e-6
NEG = -1e30
ROPE_THETA = 10000.0

LANES = 128
MXU = 256
VMEM_LIMIT = 52 * 1024 * 1024

CHUNK = 64
LEFT_CHUNKS = 8
REL_CLIP = 128
MLA_HEADS = 16
MLA_NOPE = 64
MLA_ROPE = 32
MLA_QK = MLA_NOPE + MLA_ROPE
MLA_V = 64
CA_HEADS = 16
CA_DIM = 64
TOP_K = 2


def _params(sem):
    return pltpu.CompilerParams(dimension_semantics=sem, vmem_limit_bytes=VMEM_LIMIT)


def _rms_rows(x, g):
    ms = jnp.mean(x * x, axis=-1, keepdims=True)
    return x * lax.rsqrt(ms + EPS) * g


def _dot(a, b):
    return jnp.dot(a, b, preferred_element_type=F32)


def _dot_nt(a, b):
    return lax.dot_general(a, b, (((1,), (1,)), ((), ())), preferred_element_type=F32)


def _mla_proj_kernel(x_ref, gn_ref, wdq_ref, gcq_ref, wuq_ref, wdkv_ref, gckv_ref, wuk_ref, wuv_ref,
                     gq_ref, gkn_ref, ga_ref, gb_ref, tabq_ref, cosk_ref, sink_ref, bd_ref,
                     q_ref, k_ref, v_ref):
    kv_lora = gckv_ref.shape[1]
    h = _rms_rows(x_ref[...], gn_ref[...]).astype(BF16)
    cq = _rms_rows(_dot(h, wdq_ref[...]), gcq_ref[...]).astype(BF16)
    kva = _dot(h, wdkv_ref[...])
    ckv = _rms_rows(kva[:, :kv_lora], gckv_ref[...]).astype(BF16)
    slab_a = kva[:, kv_lora:kv_lora + LANES]
    slab_b = kva[:, kv_lora + LANES:kv_lora + 2 * LANES]
    lane = lax.broadcasted_iota(jnp.int32, slab_a.shape, 1)
    rope_lanes = (lane >= MLA_NOPE) & (lane < MLA_QK)
    kpe_ssq = jnp.sum(jnp.where(rope_lanes, slab_a * slab_a, 0.0), axis=-1, keepdims=True)
    k_rot = slab_a * (ga_ref[...] * cosk_ref[...]) + slab_b * (gb_ref[...] * sink_ref[...])
    v_ref[...] = _dot(ckv, wuv_ref[...]).astype(BF16)

    gq = gq_ref[...] * tabq_ref[...]
    gq2 = jnp.concatenate([gq, gq], axis=1)
    k_rot2 = jnp.concatenate([k_rot, k_rot], axis=1)
    gkn2 = jnp.concatenate([gkn_ref[...], gkn_ref[...]], axis=1)
    bd = bd_ref[...]
    inv_d = 1.0 / MLA_QK
    for p in range(q_ref.shape[1] // MXU):
        cols = slice(p * MXU, (p + 1) * MXU)
        qq = _dot(cq, wuq_ref[:, cols])
        ssq = _dot((qq * qq).astype(BF16), bd)
        q_ref[:, cols] = (qq * lax.rsqrt(ssq * inv_d + EPS) * gq2).astype(BF16)
        kk = _dot(ckv, wuk_ref[:, cols])
        ssq = _dot((kk * kk).astype(BF16), bd) + kpe_ssq
        k_ref[:, cols] = ((kk * gkn2 + k_rot2) * lax.rsqrt(ssq * inv_d + EPS)).astype(BF16)


def _mla_proj(x2d, seq, g_attn, w_dq, g_cq, w_uq, w_dkv, g_ckv, w_ukv, g_qn, g_kn):
    T, D = x2d.shape
    H, NOPE, ROPE, QK, VD = MLA_HEADS, MLA_NOPE, MLA_ROPE, MLA_QK, MLA_V
    half = ROPE // 2
    q_lora = w_dq.shape[1]
    kv_lora = g_ckv.shape[0]
    tm = min(512, seq)

    wq = w_uq.reshape(q_lora, H, QK)
    wq_n, wq_1, wq_2 = wq[..., :NOPE], wq[..., NOPE:NOPE + half], wq[..., NOPE + half:]
    wuq_p = jnp.concatenate([wq_n, wq_1, wq_2, wq_2, wq_1], axis=-1).reshape(q_lora, H * LANES).astype(BF16)
    wkv = w_ukv.reshape(kv_lora, H, NOPE + VD)
    wuk_p = jnp.concatenate([wkv[..., :NOPE], jnp.zeros((kv_lora, H, LANES - NOPE), F32)], axis=-1)
    wuk_p = wuk_p.reshape(kv_lora, H * LANES).astype(BF16)
    wuv_p = wkv[..., NOPE:].reshape(kv_lora, H * VD).astype(BF16)
    pe1, pe2 = w_dkv[:, kv_lora:kv_lora + half], w_dkv[:, kv_lora + half:]
    z = jnp.zeros((D, NOPE), F32)
    wdkv_p = jnp.concatenate([w_dkv[:, :kv_lora], z, pe1, pe2, pe1, pe2, z, pe2, pe1, pe2, pe1], axis=1).astype(BF16)

    gq = jnp.concatenate([g_qn[:NOPE], g_qn[NOPE:NOPE + half], g_qn[NOPE + half:],
                          g_qn[NOPE + half:], g_qn[NOPE:NOPE + half]])[None]
    k1, k2 = g_kn[NOPE:NOPE + half], g_kn[NOPE + half:]
    zn = jnp.zeros((NOPE,), F32)
    gkn = jnp.concatenate([g_kn[:NOPE], zn])[None]
    ga = jnp.concatenate([zn, k1, k2, k1, k2])[None]
    gb = jnp.concatenate([zn, k2, k1, k2, k1])[None]

    inv = ROPE_THETA ** (-jnp.arange(half, dtype=F32) / half)
    ang = jnp.arange(seq, dtype=F32)[:, None] * inv[None, :]
    c, s = jnp.cos(ang), jnp.sin(ang)
    ones = jnp.ones((seq, NOPE), F32)
    zs = jnp.zeros((seq, NOPE), F32)
    tabq = jnp.concatenate([ones, c, c, -s, s], axis=1) * (QK ** -0.5)
    cosk = jnp.concatenate([zs, c, c, c, c], axis=1)
    sink = jnp.concatenate([zs, -s, s, -s, s], axis=1)

    r = np.arange(MXU)
    bd = ((r[:, None] // LANES == r[None, :] // LANES) & (r[:, None] % LANES < QK)).astype(np.float32)
    bd = jnp.asarray(bd, BF16)

    nseq = seq // tm
    row = lambda i: (i, 0)
    fixed = lambda i: (0, 0)
    pos = lambda i: (i % nseq, 0)
    full = lambda a: pl.BlockSpec(a.shape, fixed)
    ins = [x2d, g_attn[None], w_dq.astype(BF16), g_cq[None], wuq_p, wdkv_p, g_ckv[None], wuk_p, wuv_p,
           gq, gkn, ga, gb, tabq, cosk, sink, bd]
    in_specs = [pl.BlockSpec((tm, D), row)] + [full(a) for a in ins[1:13]]
    in_specs += [pl.BlockSpec((tm, LANES), pos)] * 3 + [full(bd)]
    return pl.pallas_call(
        _mla_proj_kernel,
        grid=(T // tm,),
        in_specs=in_specs,
        out_specs=[pl.BlockSpec((tm, H * LANES), row), pl.BlockSpec((tm, H * LANES), row),
                   pl.BlockSpec((tm, H * VD), row)],
        out_shape=[jax.ShapeDtypeStruct((T, H * LANES), BF16), jax.ShapeDtypeStruct((T, H * LANES), BF16),
                   jax.ShapeDtypeStruct((T, H * VD), BF16)],
        compiler_params=_params(("arbitrary",)),
        name="mla_proj",
    )(*ins)


def _mla_attn_kernel(q_ref, k_ref, v_ref, o_ref, vt_ref, *, tq):
    qi = pl.program_id(2)
    nk = vt_ref.shape[0]

    @pl.when(qi == 0)
    def _():
        for j in range(nk):
            vt_ref[j] = v_ref[0, j * tq:(j + 1) * tq, :].astype(F32).T.astype(BF16)

    krow = lax.broadcasted_iota(jnp.int32, (tq, tq), 0) // CHUNK
    qcol = lax.broadcasted_iota(jnp.int32, (tq, tq), 1) // CHUNK
    diag_ok = krow <= qcol
    for hh in range(2):
        hs = slice(hh * LANES, (hh + 1) * LANES)
        vs = slice(hh * MLA_V, (hh + 1) * MLA_V)
        q = q_ref[0, :, hs]

        def step(j, carry, masked):
            m, l, acc = carry
            start = pl.multiple_of(j * tq, tq)
            s = _dot_nt(k_ref[0, pl.ds(start, tq), hs], q)
            if masked:
                s = jnp.where(diag_ok, s, NEG)
            m_new = jnp.maximum(m, jnp.max(s, axis=0, keepdims=True))
            p = jnp.exp(s - m_new)
            alpha = jnp.exp(m - m_new)
            l = alpha * l + jnp.sum(p, axis=0, keepdims=True)
            acc = alpha * acc + _dot(vt_ref[j, vs, :], p.astype(BF16))
            return m_new, l, acc

        init = (jnp.full((1, tq), NEG, F32), jnp.zeros((1, tq), F32), jnp.zeros((MLA_V, tq), F32))
        carry = lax.fori_loop(0, qi, functools.partial(step, masked=False), init)
        m, l, acc = step(qi, carry, masked=True)
        o_ref[0, :, vs] = (acc / l).T.astype(BF16)


def _mla_attn(q, k, v, batch, seq):
    H = MLA_HEADS
    tq = min(256, seq)
    q3 = q.reshape(batch, seq, H * LANES)
    k3 = k.reshape(batch, seq, H * LANES)
    v3 = v.reshape(batch, seq, H * MLA_V)
    out = pl.pallas_call(
        functools.partial(_mla_attn_kernel, tq=tq),
        grid=(batch, H // 2, seq // tq),
        in_specs=[pl.BlockSpec((1, tq, 2 * LANES), lambda b, h, i: (b, i, h)),
                  pl.BlockSpec((1, seq, 2 * LANES), lambda b, h, i: (b, 0, h)),
                  pl.BlockSpec((1, seq, 2 * MLA_V), lambda b, h, i: (b, 0, h))],
        out_specs=pl.BlockSpec((1, tq, 2 * MLA_V), lambda b, h, i: (b, i, h)),
        out_shape=jax.ShapeDtypeStruct((batch, seq, H * MLA_V), BF16),
        scratch_shapes=[pltpu.VMEM((seq // tq, 2 * MLA_V, tq), BF16)],
        compiler_params=_params(("arbitrary", "arbitrary", "arbitrary")),
        name="mla_attn",
    )(q3, k3, v3)
    return out.reshape(batch * seq, H * MLA_V)


def _proj_res_norm_kernel(a_ref, w_ref, x_ref, g_ref, xo_ref, ho_ref):
    y = x_ref[...] + _dot(a_ref[...], w_ref[...])
    xo_ref[...] = y
    ho_ref[...] = _rms_rows(y, g_ref[...]).astype(BF16)


def _proj_res_norm(a, w, x2d, g, seq):
    T, D = x2d.shape
    tm = min(512, seq)
    row = lambda i: (i, 0)
    fixed = lambda i: (0, 0)
    return pl.pallas_call(
        _proj_res_norm_kernel,
        grid=(T // tm,),
        in_specs=[pl.BlockSpec((tm, a.shape[1]), row), pl.BlockSpec(w.shape, fixed),
                  pl.BlockSpec((tm, D), row), pl.BlockSpec((1, D), fixed)],
        out_specs=[pl.BlockSpec((tm, D), row), pl.BlockSpec((tm, D), row)],
        out_shape=[jax.ShapeDtypeStruct((T, D), F32), jax.ShapeDtypeStruct((T, D), BF16)],
        compiler_params=_params(("arbitrary",)),
        name="proj_res_norm",
    )(a, w.astype(BF16), x2d, g[None])


def _ffn_kernel(lo_ref, hi_ref, first_ref, np_ref, h_ref, w1_ref, w3_ref, w2_ref, *rest, residual):
    if residual:
        res_ref, o_ref, acc_ref = rest
    else:
        o_ref, acc_ref = rest
    i, c = pl.program_id(0), pl.program_id(1)

    @pl.when(i < np_ref[0])
    def _():
        @pl.when(c == 0)
        def _():
            acc_ref[...] = jnp.zeros_like(acc_ref)

        h = h_ref[...].astype(BF16)
        u = _dot(h, w1_ref[0])
        g = _dot(h, w3_ref[0])
        a = (u * jax.nn.sigmoid(u) * g).astype(BF16)
        acc_ref[...] += _dot(a, w2_ref[0])

        last = c == pl.num_programs(1) - 1

        @pl.when(last & (first_ref[i] == 1))
        def _():
            if residual:
                o_ref[...] = res_ref[...] + acc_ref[...]
            else:
                o_ref[...] = acc_ref[...]

        @pl.when(last & (first_ref[i] == 0))
        def _():
            rows = lax.broadcasted_iota(jnp.int32, o_ref.shape, 0)
            mine = (rows >= lo_ref[i]) & (rows < hi_ref[i])
            val = res_ref[...] + acc_ref[...] if residual else acc_ref[...]
            o_ref[...] = jnp.where(mine, val, o_ref[...])


def _ffn_chunk(ff):
    for n in range(1, ff // LANES + 1):
        if ff % n == 0 and (ff // n) % LANES == 0 and ff // n <= 1536:
            return ff // n
    return LANES


def _ffn(h, w1, w3, w2, sched, rt, residual=None):
    R, D = h.shape
    E, _, FF = w1.shape
    fc = _ffn_chunk(FF)
    expert, tile, lo, hi, first, count = sched
    hmap = lambda i, c, te, tb, *_: (tb[i], 0)
    in_specs = [pl.BlockSpec((rt, D), hmap),
                pl.BlockSpec((1, D, fc), lambda i, c, te, *_: (te[i], 0, c)),
                pl.BlockSpec((1, D, fc), lambda i, c, te, *_: (te[i], 0, c)),
                pl.BlockSpec((1, fc, D), lambda i, c, te, *_: (te[i], c, 0))]
    ins = [h, w1, w3, w2]
    if residual is not None:
        in_specs.append(pl.BlockSpec((rt, D), hmap))
        ins.append(residual)

    def kern(te_ref, tb_ref, *refs):
        _ffn_kernel(*refs, residual=residual is not None)

    return pl.pallas_call(
        kern,
        grid_spec=pltpu.PrefetchScalarGridSpec(
            num_scalar_prefetch=6,
            grid=(expert.shape[0], FF // fc),
            in_specs=in_specs,
            out_specs=pl.BlockSpec((rt, D), hmap),
            scratch_shapes=[pltpu.VMEM((rt, D), F32)],
        ),
        out_shape=jax.ShapeDtypeStruct((R, D), F32),
        compiler_params=_params(("arbitrary", "arbitrary")),
        name="ffn",
    )(expert, tile, lo, hi, first, count, *ins)


def _dense_schedule(n_tiles, rt):
    i32 = jnp.int32
    return (jnp.zeros((n_tiles,), i32), jnp.arange(n_tiles, dtype=i32), jnp.zeros((n_tiles,), i32),
            jnp.full((n_tiles,), rt, i32), jnp.ones((n_tiles,), i32), jnp.full((1,), n_tiles, i32))


def _expert_schedule(cnt, n_rows, rt):
    i32 = jnp.int32
    E = cnt.shape[0]
    end = jnp.cumsum(cnt)
    off = end - cnt
    t_first = off // rt
    t_last = jnp.maximum(end - 1, off) // rt
    n_vis = jnp.where(cnt > 0, t_last - t_first + 1, 0)
    vis_end = jnp.cumsum(n_vis)
    count = vis_end[-1]
    max_vis = n_rows // rt + E - 1
    p = jnp.minimum(jnp.arange(max_vis, dtype=i32), jnp.maximum(count - 1, 0))
    expert = jnp.minimum(jnp.searchsorted(vis_end, p, side="right"), E - 1).astype(i32)
    tile = t_first[expert] + (p - (vis_end[expert] - n_vis[expert]))
    lo = jnp.clip(off[expert] - tile * rt, 0, rt)
    hi = jnp.clip(end[expert] - tile * rt, 0, rt)
    first = jnp.concatenate([jnp.ones((1,), i32), (tile[1:] != tile[:-1]).astype(i32)])
    return (expert, tile.astype(i32), lo.astype(i32), hi.astype(i32), first, count.reshape(1).astype(i32)), off


def _band_proj_kernel(x_ref, gkv_ref, gq_ref, wk_ref, wv_ref, wq_ref, gkn_ref, gqn_ref, bd_ref,
                      k_ref, v_ref, q_ref, *, npad):
    j = pl.program_id(1)

    @pl.when(j < npad)
    def _():
        k_ref[...] = jnp.zeros_like(k_ref)
        v_ref[...] = jnp.zeros_like(v_ref)

    @pl.when(j >= npad)
    def _():
        x = x_ref[0]
        hk = _rms_rows(x, gkv_ref[...]).astype(BF16)
        hq = _rms_rows(x, gq_ref[...]).astype(BF16)
        v_ref[0] = _dot(hk, wv_ref[...]).astype(BF16)
        bd = bd_ref[...]
        inv_d = 1.0 / CA_DIM
        for p in range(k_ref.shape[2] // MXU):
            cols = slice(p * MXU, (p + 1) * MXU)
            kk = _dot(hk, wk_ref[:, cols])
            ssq = _dot((kk * kk).astype(BF16), bd)
            k_ref[0, :, cols] = (kk * lax.rsqrt(ssq * inv_d + EPS) * gkn_ref[...]).astype(BF16)
            qq = _dot(hq, wq_ref[:, cols])
            ssq = _dot((qq * qq).astype(BF16), bd)
            q_ref[0, :, cols] = (qq * lax.rsqrt(ssq * inv_d + EPS) * gqn_ref[...]).astype(BF16)


def _band_proj(x2d, batch, seq, g_kv, g_attn, w_k, w_v, w_q, g_kn, g_qn):
    D = x2d.shape[1]
    HD = CA_HEADS * CA_DIM
    pad = LEFT_CHUNKS * CHUNK
    tm = min(512, seq)
    npad = pad // tm
    x3 = x2d.reshape(batch, seq, D)
    r = np.arange(MXU)
    bd = jnp.asarray((r[:, None] // CA_DIM == r[None, :] // CA_DIM).astype(np.float32), BF16)
    gkn = jnp.tile(g_kn, MXU // CA_DIM)[None]
    gqn = jnp.tile(g_qn, MXU // CA_DIM)[None] * (CA_DIM ** -0.5)
    fixed = lambda b, j: (0, 0)
    src = lambda b, j: (b, jnp.maximum(j - npad, 0), 0)
    return pl.pallas_call(
        functools.partial(_band_proj_kernel, npad=npad),
        grid=(batch, npad + seq // tm),
        in_specs=[pl.BlockSpec((1, tm, D), src), pl.BlockSpec((1, D), fixed), pl.BlockSpec((1, D), fixed),
                  pl.BlockSpec((D, HD), fixed), pl.BlockSpec((D, HD), fixed), pl.BlockSpec((D, HD), fixed),
                  pl.BlockSpec((1, MXU), fixed), pl.BlockSpec((1, MXU), fixed), pl.BlockSpec((MXU, MXU), fixed)],
        out_specs=[pl.BlockSpec((1, tm, HD), lambda b, j: (b, j, 0)),
                   pl.BlockSpec((1, tm, HD), lambda b, j: (b, j, 0)),
                   pl.BlockSpec((1, tm, HD), src)],
        out_shape=[jax.ShapeDtypeStruct((batch, seq + pad, HD), BF16),
                   jax.ShapeDtypeStruct((batch, seq + pad, HD), BF16),
                   jax.ShapeDtypeStruct((batch, seq, HD), BF16)],
        compiler_params=_params(("arbitrary", "arbitrary")),
        name="band_proj",
    )(x3, g_kv[None], g_attn[None], w_k.astype(BF16), w_v.astype(BF16), w_q.astype(BF16), gkn, gqn, bd)


def _band_attn_kernel(q_ref, k_ref, v_ref, bias_ref, o_ref, vt_ref, *, tq, pad):
    qi = pl.program_id(2)
    nblk = vt_ref.shape[0]
    nwin = (tq + pad) // tq

    @pl.when(qi == 0)
    def _():
        for j in range(nblk):
            vt_ref[j] = v_ref[0, j * tq:(j + 1) * tq, :].astype(F32).T.astype(BF16)

    win = tq + pad
    start = pl.multiple_of(qi * tq, tq)
    kwin = k_ref[0, pl.ds(start, win), :]
    q2 = q_ref[0]
    lane = lax.broadcasted_iota(jnp.int32, q2.shape, 1)
    key_pos = lax.broadcasted_iota(jnp.int32, (win, tq), 0) + qi * tq
    valid = key_pos >= pad
    for hh in range(2):
        vs = slice(hh * CA_DIM, (hh + 1) * CA_DIM)
        qh = jnp.where((lane >= hh * CA_DIM) & (lane < (hh + 1) * CA_DIM), q2, jnp.zeros_like(q2))
        s = _dot_nt(kwin, qh) + bias_ref[hh]
        s = jnp.where(valid, s, NEG)
        m = jnp.max(s, axis=0, keepdims=True)
        p = jnp.exp(s - m)
        l = jnp.sum(p, axis=0, keepdims=True)
        pb = p.astype(BF16)
        acc = jnp.zeros((CA_DIM, tq), F32)
        for w in range(nwin):
            acc = acc + _dot(vt_ref[qi + w, vs, :], pb[w * tq:(w + 1) * tq, :])
        o_ref[0, :, vs] = (acc / l).T.astype(BF16)


def _band_bias(rel_bias, tq):
    pad = LEFT_CHUNKS * CHUNK
    win = tq + pad
    r = np.arange(win)[:, None]
    c = np.arange(tq)[None, :]
    rel = c + pad - r
    idx = np.clip(rel, -REL_CLIP, REL_CLIP) + REL_CLIP
    qc = (c + pad) // CHUNK
    kc = r // CHUNK
    vis = (kc <= qc) & (kc >= qc - LEFT_CHUNKS)
    return jnp.where(jnp.asarray(vis)[None], rel_bias[:, idx].astype(F32), NEG)


def _band_attn(q, k_pad, v_pad, rel_bias, batch, seq):
    H = CA_HEADS
    HD = H * CA_DIM
    pad = LEFT_CHUNKS * CHUNK
    tq = min(256, seq)
    win = tq + pad
    bias = _band_bias(rel_bias, tq)
    out = pl.pallas_call(
        functools.partial(_band_attn_kernel, tq=tq, pad=pad),
        grid=(H // 2, batch, seq // tq),
        in_specs=[pl.BlockSpec((1, tq, LANES), lambda h, b, i: (b, i, h)),
                  pl.BlockSpec((1, seq + pad, LANES), lambda h, b, i: (b, 0, h)),
                  pl.BlockSpec((1, seq + pad, LANES), lambda h, b, i: (b, 0, h)),
                  pl.BlockSpec((2, win, tq), lambda h, b, i: (h, 0, 0))],
        out_specs=pl.BlockSpec((1, tq, LANES), lambda h, b, i: (b, i, h)),
        out_shape=jax.ShapeDtypeStruct((batch, seq, HD), BF16),
        scratch_shapes=[pltpu.VMEM(((seq + pad) // tq, LANES, tq), BF16)],
        compiler_params=_params(("arbitrary", "arbitrary", "arbitrary")),
        name="band_attn",
    )(q, k_pad, v_pad, bias)
    return out.reshape(batch * seq, HD)


R_E1, R_E2, R_RANK1, R_RANK2, R_G1, R_G2 = 8, 9, 10, 11, 12, 13


def _router_kernel(a_ref, w_ref, x_ref, g_ref, wrh_ref, wrl_ref, xo_ref, ho_ref, r_ref, cnt_ref, run_ref,
                   *, n_experts):
    i = pl.program_id(0)

    @pl.when(i == 0)
    def _():
        run_ref[...] = jnp.zeros_like(run_ref)

    y = x_ref[...] + _dot(a_ref[...], w_ref[...])
    xo_ref[...] = y
    hf = _rms_rows(y, g_ref[...])
    ho_ref[...] = hf
    h_hi = hf.astype(BF16)
    h_lo = (hf - h_hi.astype(F32)).astype(BF16)
    logits = _dot(h_hi, wrh_ref[...]) + (_dot(h_lo, wrh_ref[...]) + _dot(h_hi, wrl_ref[...]))
    tm = logits.shape[0]
    lane = lax.broadcasted_iota(jnp.int32, logits.shape, 1).astype(F32)
    logits = jnp.where(lane < n_experts, logits, NEG)
    m1 = jnp.max(logits, axis=-1, keepdims=True)
    e1 = jnp.min(jnp.where(logits == m1, lane, float(LANES)), axis=-1, keepdims=True)
    rest = jnp.where(lane == e1, NEG, logits)
    m2 = jnp.max(rest, axis=-1, keepdims=True)
    e2 = jnp.min(jnp.where(rest == m2, lane, float(LANES)), axis=-1, keepdims=True)
    ex = jnp.exp(m2 - m1)
    g1 = 1.0 / (1.0 + ex)
    g2 = ex * g1
    oh1 = (lane == e1).astype(F32)
    oh2 = (lane == e2).astype(F32)
    oh = oh1 + oh2
    tri = (lax.broadcasted_iota(jnp.int32, (tm, tm), 0) > lax.broadcasted_iota(jnp.int32, (tm, tm), 1))
    before = _dot(tri.astype(BF16), oh.astype(BF16)) + run_ref[...]
    rank1 = jnp.sum(oh1 * before, axis=-1, keepdims=True)
    rank2 = jnp.sum(oh2 * before, axis=-1, keepdims=True)
    run_ref[...] += jnp.sum(oh, axis=0, keepdims=True)
    rec = oh1 * g1 + oh2 * g2
    for ln, val in ((R_E1, e1), (R_E2, e2), (R_RANK1, rank1), (R_RANK2, rank2), (R_G1, g1), (R_G2, g2)):
        rec = jnp.where(lane == ln, val, rec)
    r_ref[...] = rec
    cnt_ref[...] = run_ref[...]


def _router(a, w, x2d, g, w_router, seq):
    T, D = x2d.shape
    E = w_router.shape[1]
    tm = min(512, seq)
    wr = jnp.zeros((D, LANES), F32).at[:, :E].set(w_router)
    wr_hi = wr.astype(BF16)
    wr_lo = (wr - wr_hi.astype(F32)).astype(BF16)
    row = lambda i: (i, 0)
    fixed = lambda i: (0, 0)
    return pl.pallas_call(
        functools.partial(_router_kernel, n_experts=E),
        grid=(T // tm,),
        in_specs=[pl.BlockSpec((tm, a.shape[1]), row), pl.BlockSpec(w.shape, fixed),
                  pl.BlockSpec((tm, D), row), pl.BlockSpec((1, D), fixed),
                  pl.BlockSpec((D, LANES), fixed), pl.BlockSpec((D, LANES), fixed)],
        out_specs=[pl.BlockSpec((tm, D), row), pl.BlockSpec((tm, D), row),
                   pl.BlockSpec((tm, LANES), row), pl.BlockSpec((1, LANES), fixed)],
        out_shape=[jax.ShapeDtypeStruct((T, D), F32), jax.ShapeDtypeStruct((T, D), F32),
                   jax.ShapeDtypeStruct((T, LANES), F32), jax.ShapeDtypeStruct((1, LANES), F32)],
        scratch_shapes=[pltpu.VMEM((1, LANES), F32)],
        compiler_params=_params(("arbitrary",)),
        name="router",
    )(a, w.astype(BF16), x2d, g[None], wr_hi, wr_lo)


def _scatter_rows_kernel(dest_ref, h_ref, o_ref, idx_ref, isem, sem, *, tm):
    i = pl.program_id(0)
    icopy = pltpu.make_async_copy(dest_ref.at[pl.ds(i * (TOP_K * tm), TOP_K * tm)], idx_ref, isem)
    icopy.start()
    icopy.wait()

    def row_copy(r, d):
        return pltpu.make_async_copy(h_ref.at[pl.ds(r, 1)], o_ref.at[pl.ds(d, 1)], sem)

    def issue(r, _):
        row_copy(r, idx_ref[r]).start()
        row_copy(r, idx_ref[tm + r]).start()
        return 0

    lax.fori_loop(0, tm, issue, 0)

    def drain(r, _):
        row_copy(0, 0).wait()
        row_copy(0, 0).wait()
        return 0

    lax.fori_loop(0, tm, drain, 0)


def _scatter_rows(h, dest, rows_out, tm):
    T, D = h.shape
    return pl.pallas_call(
        functools.partial(_scatter_rows_kernel, tm=tm),
        grid=(T // tm,),
        in_specs=[pl.BlockSpec(memory_space=pl.ANY), pl.BlockSpec((tm, D), lambda i: (i, 0))],
        out_specs=pl.BlockSpec(memory_space=pl.ANY),
        out_shape=jax.ShapeDtypeStruct((rows_out, D), h.dtype),
        scratch_shapes=[pltpu.SMEM((TOP_K * tm,), jnp.int32), pltpu.SemaphoreType.DMA, pltpu.SemaphoreType.DMA],
        compiler_params=_params(("arbitrary",)),
        name="moe_scatter",
    )(dest, h)


def _combine_kernel(dest_ref, y_ref, x_ref, r_ref, o_ref, idx_ref, b1_ref, b2_ref, isem, sem, *, tm):
    i = pl.program_id(0)
    icopy = pltpu.make_async_copy(dest_ref.at[pl.ds(i * (TOP_K * tm), TOP_K * tm)], idx_ref, isem)
    icopy.start()
    icopy.wait()

    def row_copy(d, buf, r):
        return pltpu.make_async_copy(y_ref.at[pl.ds(d, 1)], buf.at[pl.ds(r, 1)], sem)

    def issue(r, _):
        row_copy(idx_ref[r], b1_ref, r).start()
        row_copy(idx_ref[tm + r], b2_ref, r).start()
        return 0

    lax.fori_loop(0, tm, issue, 0)

    def drain(r, _):
        row_copy(0, b1_ref, 0).wait()
        row_copy(0, b2_ref, 0).wait()
        return 0

    lax.fori_loop(0, tm, drain, 0)
    rec = r_ref[...]
    lane = lax.broadcasted_iota(jnp.int32, rec.shape, 1)
    g1 = jnp.sum(jnp.where(lane == R_G1, rec, 0.0), axis=-1, keepdims=True)
    g2 = jnp.sum(jnp.where(lane == R_G2, rec, 0.0), axis=-1, keepdims=True)
    o_ref[...] = x_ref[...] + (g1 * b1_ref[...] + g2 * b2_ref[...])


def _combine(y_sorted, dest, x2d, rec, tm):
    T, D = x2d.shape
    row = lambda i: (i, 0)
    return pl.pallas_call(
        functools.partial(_combine_kernel, tm=tm),
        grid=(T // tm,),
        in_specs=[pl.BlockSpec(memory_space=pl.ANY), pl.BlockSpec(memory_space=pl.ANY),
                  pl.BlockSpec((tm, D), row), pl.BlockSpec((tm, LANES), row)],
        out_specs=pl.BlockSpec((tm, D), row),
        out_shape=jax.ShapeDtypeStruct((T, D), F32),
        scratch_shapes=[pltpu.SMEM((TOP_K * tm,), jnp.int32), pltpu.VMEM((tm, D), F32), pltpu.VMEM((tm, D), F32),
                        pltpu.SemaphoreType.DMA, pltpu.SemaphoreType.DMA],
        compiler_params=_params(("arbitrary",)),
        name="moe_combine",
    )(dest, y_sorted, x2d, rec)


def _moe(h, x2d, rec, counts, w1, w3, w2, seq):
    T, D = h.shape
    E = w1.shape[0]
    rt = min(1024, T)
    tm = min(512, seq)
    cnt = counts[0, :E].astype(jnp.int32)
    sched, group_start = _expert_schedule(cnt, TOP_K * T, rt)
    e1 = rec[:, R_E1].astype(jnp.int32)
    e2 = rec[:, R_E2].astype(jnp.int32)
    d1 = group_start[e1] + rec[:, R_RANK1].astype(jnp.int32)
    d2 = group_start[e2] + rec[:, R_RANK2].astype(jnp.int32)
    dest = jnp.concatenate([d1.reshape(T // tm, tm), d2.reshape(T // tm, tm)], axis=1).reshape(-1)

    h_sorted = _scatter_rows(h, dest, TOP_K * T, tm)
    y_sorted = _ffn(h_sorted, w1, w3, w2, sched, rt)
    return _combine(y_sorted, dest, x2d, rec, tm)


def kernel(x, attn_norm, ffn_norm, a_w_dq, a_g_cq, a_w_uq, a_w_dkv, a_g_ckv, a_w_ukv, a_g_qn, a_g_kn, a_w_o,
           kv_norm, kv_w_k, kv_w_v, kv_g_kn, b_w_q, b_g_qn, b_rel_bias, b_w_o, ffn_w1, ffn_w3, ffn_w2,
           moe_router, moe_w1, moe_w3, moe_w2):
    B, S, D = x.shape
    T = B * S
    assert S % 256 == 0 and T % 512 == 0 and (TOP_K * T) % min(1024, T) == 0
    x0 = x.reshape(T, D)

    q, k, v = _mla_proj(x0, S, attn_norm[0], a_w_dq[0], a_g_cq[0], a_w_uq[0], a_w_dkv[0], a_g_ckv[0],
                        a_w_ukv[0], a_g_qn[0], a_g_kn[0])
    o = _mla_attn(q, k, v, B, S)
    x1, h1 = _proj_res_norm(o, a_w_o[0], x0, ffn_norm[0], S)
    rt = min(512, T)
    x2 = _ffn(h1, ffn_w1.astype(BF16), ffn_w3.astype(BF16), ffn_w2.astype(BF16),
              _dense_schedule(T // rt, rt), rt, residual=x1)

    k_pad, v_pad, qb = _band_proj(x2, B, S, kv_norm, attn_norm[1], kv_w_k, kv_w_v, b_w_q[0], kv_g_kn, b_g_qn[0])
    ob = _band_attn(qb, k_pad, v_pad, b_rel_bias[0], B, S)
    x3, h3, rec, counts = _router(ob, b_w_o[0], x2, ffn_norm[1], moe_router[0], S)
    x4 = _moe(h3, x3, rec, counts, moe_w1[0].astype(BF16), moe_w3[0].astype(BF16), moe_w2[0].astype(BF16), S)
    return x4.reshape(B, S, D)
```

```python
import functools

import jax
import jax.numpy as jnp
import numpy as np
from jax import lax
from jax.experimental import pallas as pl
from jax.experimental.pallas import tpu as pltpu

F32 = jnp.float32
BF16 = jnp.bfloat16

EPS = 1e-6
NEG = -1e30
ROPE_THETA = 10000.0
LOG2E = 1.4426950408889634

LANES = 128
MXU = 256
VMEM_LIMIT = 52 * 1024 * 1024

CHUNK = 64
LEFT_CHUNKS = 8
REL_CLIP = 128
MLA_HEADS = 16
MLA_NOPE = 64
MLA_ROPE = 32
MLA_QK = MLA_NOPE + MLA_ROPE
MLA_V = 64
CA_HEADS = 16
CA_DIM = 64
TOP_K = 2


def _params(sem):
    return pltpu.CompilerParams(dimension_semantics=sem, vmem_limit_bytes=VMEM_LIMIT)


def _rms_rows(x, g):
    ms = jnp.mean(x * x, axis=-1, keepdims=True)
    return x * lax.rsqrt(ms + EPS) * g


def _dot(a, b):
    return jnp.dot(a, b, preferred_element_type=F32)


def _dot_nt(a, b):
    return lax.dot_general(a, b, (((1,), (1,)), ((), ())), preferred_element_type=F32)


def _mla_proj_kernel(x_ref, gn_ref, wdq_ref, gcq_ref, wuq_ref, wdkv_ref, gckv_ref, wuk_ref, wuv_ref,
                     gq_ref, gkn_ref, ga_ref, gb_ref, tabq_ref, cosk_ref, sink_ref, bd_ref,
                     q_ref, k_ref, v_ref):
    kv_lora = gckv_ref.shape[1]
    h = _rms_rows(x_ref[...], gn_ref[...]).astype(BF16)
    cq = _rms_rows(_dot(h, wdq_ref[...]), gcq_ref[...]).astype(BF16)
    kva = _dot(h, wdkv_ref[...])
    ckv = _rms_rows(kva[:, :kv_lora], gckv_ref[...]).astype(BF16)
    slab_a = kva[:, kv_lora:kv_lora + LANES]
    slab_b = kva[:, kv_lora + LANES:kv_lora + 2 * LANES]
    lane = lax.broadcasted_iota(jnp.int32, slab_a.shape, 1)
    rope_lanes = (lane >= MLA_NOPE) & (lane < MLA_QK)
    kpe_ssq = jnp.sum(jnp.where(rope_lanes, slab_a * slab_a, 0.0), axis=-1, keepdims=True)
    k_rot = slab_a * (ga_ref[...] * cosk_ref[...]) + slab_b * (gb_ref[...] * sink_ref[...])
    v_ref[...] = _dot(ckv, wuv_ref[...]).astype(BF16)

    gq = gq_ref[...] * tabq_ref[...]
    gq2 = jnp.concatenate([gq, gq], axis=1)
    k_rot2 = jnp.concatenate([k_rot, k_rot], axis=1)
    gkn2 = jnp.concatenate([gkn_ref[...], gkn_ref[...]], axis=1)
    bd = bd_ref[...]
    inv_d = 1.0 / MLA_QK
    for p in range(q_ref.shape[1] // MXU):
        cols = slice(p * MXU, (p + 1) * MXU)
        qq = _dot(cq, wuq_ref[:, cols])
        ssq = _dot((qq * qq).astype(BF16), bd)
        q_ref[:, cols] = (qq * lax.rsqrt(ssq * inv_d + EPS) * gq2).astype(BF16)
        kk = _dot(ckv, wuk_ref[:, cols])
        ssq = _dot((kk * kk).astype(BF16), bd) + kpe_ssq
        k_ref[:, cols] = ((kk * gkn2 + k_rot2) * lax.rsqrt(ssq * inv_d + EPS)).astype(BF16)


def _mla_proj(x2d, seq, g_attn, w_dq, g_cq, w_uq, w_dkv, g_ckv, w_ukv, g_qn, g_kn):
    T, D = x2d.shape
    H, NOPE, ROPE, QK, VD = MLA_HEADS, MLA_NOPE, MLA_ROPE, MLA_QK, MLA_V
    half = ROPE // 2
    q_lora = w_dq.shape[1]
    kv_lora = g_ckv.shape[0]
    tm = min(512, seq)

    wq = w_uq.reshape(q_lora, H, QK)
    wq_n, wq_1, wq_2 = wq[..., :NOPE], wq[..., NOPE:NOPE + half], wq[..., NOPE + half:]
    wuq_p = jnp.concatenate([wq_n, wq_1, wq_2, wq_2, wq_1], axis=-1).reshape(q_lora, H * LANES).astype(BF16)
    wkv = w_ukv.reshape(kv_lora, H, NOPE + VD)
    wuk_p = jnp.concatenate([wkv[..., :NOPE], jnp.zeros((kv_lora, H, LANES - NOPE), F32)], axis=-1)
    wuk_p = wuk_p.reshape(kv_lora, H * LANES).astype(BF16)
    wuv_p = wkv[..., NOPE:].reshape(kv_lora, H * VD).astype(BF16)
    pe1, pe2 = w_dkv[:, kv_lora:kv_lora + half], w_dkv[:, kv_lora + half:]
    z = jnp.zeros((D, NOPE), F32)
    wdkv_p = jnp.concatenate([w_dkv[:, :kv_lora], z, pe1, pe2, pe1, pe2, z, pe2, pe1, pe2, pe1], axis=1).astype(BF16)

    gq = jnp.concatenate([g_qn[:NOPE], g_qn[NOPE:NOPE + half], g_qn[NOPE + half:],
                          g_qn[NOPE + half:], g_qn[NOPE:NOPE + half]])[None]
    k1, k2 = g_kn[NOPE:NOPE + half], g_kn[NOPE + half:]
    zn = jnp.zeros((NOPE,), F32)
    gkn = jnp.concatenate([g_kn[:NOPE], zn])[None]
    ga = jnp.concatenate([zn, k1, k2, k1, k2])[None]
    gb = jnp.concatenate([zn, k2, k1, k2, k1])[None]

    inv = ROPE_THETA ** (-jnp.arange(half, dtype=F32) / half)
    ang = jnp.arange(seq, dtype=F32)[:, None] * inv[None, :]
    c, s = jnp.cos(ang), jnp.sin(ang)
    ones = jnp.ones((seq, NOPE), F32)
    zs = jnp.zeros((seq, NOPE), F32)
    tabq = jnp.concatenate([ones, c, c, -s, s], axis=1) * (QK ** -0.5 * LOG2E)
    cosk = jnp.concatenate([zs, c, c, c, c], axis=1)
    sink = jnp.concatenate([zs, -s, s, -s, s], axis=1)

    r = np.arange(MXU)
    bd = ((r[:, None] // LANES == r[None, :] // LANES) & (r[:, None] % LANES < QK)).astype(np.float32)
    bd = jnp.asarray(bd, BF16)

    nseq = seq // tm
    row = lambda i: (i, 0)
    fixed = lambda i: (0, 0)
    pos = lambda i: (i % nseq, 0)
    full = lambda a: pl.BlockSpec(a.shape, fixed)
    ins = [x2d, g_attn[None], w_dq.astype(BF16), g_cq[None], wuq_p, wdkv_p, g_ckv[None], wuk_p, wuv_p,
           gq, gkn, ga, gb, tabq, cosk, sink, bd]
    in_specs = [pl.BlockSpec((tm, D), row)] + [full(a) for a in ins[1:13]]
    in_specs += [pl.BlockSpec((tm, LANES), pos)] * 3 + [full(bd)]
    return pl.pallas_call(
        _mla_proj_kernel,
        grid=(T // tm,),
        in_specs=in_specs,
        out_specs=[pl.BlockSpec((tm, H * LANES), row), pl.BlockSpec((tm, H * LANES), row),
                   pl.BlockSpec((tm, H * VD), row)],
        out_shape=[jax.ShapeDtypeStruct((T, H * LANES), BF16), jax.ShapeDtypeStruct((T, H * LANES), BF16),
                   jax.ShapeDtypeStruct((T, H * VD), BF16)],
        compiler_params=_params(("arbitrary",)),
        name="mla_proj",
    )(*ins)


def _mla_attn_kernel(q_ref, k_ref, v_ref, o_ref, vt_ref, *, tq):
    seq = q_ref.shape[1]
    nq = seq // tq
    for j in range(nq):
        blk = slice(j * tq, (j + 1) * tq)
        vt_ref[:, blk] = v_ref[0, blk, :].astype(F32).T.astype(BF16)

    krow = lax.broadcasted_iota(jnp.int32, (tq, tq), 0) // CHUNK
    qcol = lax.broadcasted_iota(jnp.int32, (tq, tq), 1) // CHUNK
    diag_ok = krow <= qcol
    for qi in range(nq):
        rows = slice(qi * tq, (qi + 1) * tq)
        past = slice(0, qi * tq)
        for hh in range(2):
            hs = slice(hh * LANES, (hh + 1) * LANES)
            vs = slice(hh * MLA_V, (hh + 1) * MLA_V)
            q = q_ref[0, rows, hs]
            sd = jnp.where(diag_ok, _dot_nt(k_ref[0, rows, hs], q), NEG)
            m = jnp.max(sd, axis=0, keepdims=True)
            if qi:
                so = _dot_nt(k_ref[0, past, hs], q)
                m = jnp.maximum(m, jnp.max(so, axis=0, keepdims=True))
            pd = jnp.exp2(sd - m)
            l = jnp.sum(pd, axis=0, keepdims=True)
            acc = _dot(vt_ref[vs, rows], pd.astype(BF16))
            if qi:
                po = jnp.exp2(so - m)
                l = l + jnp.sum(po, axis=0, keepdims=True)
                acc = acc + _dot(vt_ref[vs, past], po.astype(BF16))
            o_ref[0, rows, vs] = (acc * (1.0 / l)).T.astype(BF16)


def _mla_attn(q, k, v, batch, seq):
    H = MLA_HEADS
    q3 = q.reshape(batch, seq, H * LANES)
    k3 = k.reshape(batch, seq, H * LANES)
    v3 = v.reshape(batch, seq, H * MLA_V)
    pair = lambda b, h: (b, 0, h)
    out = pl.pallas_call(
        functools.partial(_mla_attn_kernel, tq=min(256, seq)),
        grid=(batch, H // 2),
        in_specs=[pl.BlockSpec((1, seq, 2 * LANES), pair), pl.BlockSpec((1, seq, 2 * LANES), pair),
                  pl.BlockSpec((1, seq, 2 * MLA_V), pair)],
        out_specs=pl.BlockSpec((1, seq, 2 * MLA_V), pair),
        out_shape=jax.ShapeDtypeStruct((batch, seq, H * MLA_V), BF16),
        scratch_shapes=[pltpu.VMEM((2 * MLA_V, seq), BF16)],
        compiler_params=_params(("arbitrary", "arbitrary")),
        name="mla_attn",
    )(q3, k3, v3)
    return out.reshape(batch * seq, H * MLA_V)


def _proj_res_norm_kernel(a_ref, w_ref, x_ref, g_ref, xo_ref, ho_ref):
    y = x_ref[...] + _dot(a_ref[...], w_ref[...])
    xo_ref[...] = y
    ho_ref[...] = _rms_rows(y, g_ref[...]).astype(BF16)


def _proj_res_norm(a, w, x2d, g, seq):
    T, D = x2d.shape
    tm = min(512, seq)
    row = lambda i: (i, 0)
    fixed = lambda i: (0, 0)
    return pl.pallas_call(
        _proj_res_norm_kernel,
        grid=(T // tm,),
        in_specs=[pl.BlockSpec((tm, a.shape[1]), row), pl.BlockSpec(w.shape, fixed),
                  pl.BlockSpec((tm, D), row), pl.BlockSpec((1, D), fixed)],
        out_specs=[pl.BlockSpec((tm, D), row), pl.BlockSpec((tm, D), row)],
        out_shape=[jax.ShapeDtypeStruct((T, D), F32), jax.ShapeDtypeStruct((T, D), BF16)],
        compiler_params=_params(("arbitrary",)),
        name="proj_res_norm",
    )(a, w.astype(BF16), x2d, g[None])


def _ffn_kernel(lo_ref, hi_ref, first_ref, np_ref, h_ref, w1_ref, w3_ref, w2_ref, *rest, residual):
    if residual:
        res_ref, o_ref, acc_ref = rest
    else:
        o_ref, acc_ref = rest
    i, c = pl.program_id(0), pl.program_id(1)

    @pl.when(i < np_ref[0])
    def _():
        @pl.when(c == 0)
        def _():
            acc_ref[...] = jnp.zeros_like(acc_ref)

        h = h_ref[...].astype(BF16)
        u = _dot(h, w1_ref[0])
        g = _dot(h, w3_ref[0])
        a = (u * jax.nn.sigmoid(u) * g).astype(BF16)
        acc_ref[...] += _dot(a, w2_ref[0])

        last = c == pl.num_programs(1) - 1

        @pl.when(last & (first_ref[i] == 1))
        def _():
            if residual:
                o_ref[...] = res_ref[...] + acc_ref[...]
            else:
                o_ref[...] = acc_ref[...]

        @pl.when(last & (first_ref[i] == 0))
        def _():
            rows = lax.broadcasted_iota(jnp.int32, o_ref.shape, 0)
            mine = (rows >= lo_ref[i]) & (rows < hi_ref[i])
            val = res_ref[...] + acc_ref[...] if residual else acc_ref[...]
            o_ref[...] = jnp.where(mine, val, o_ref[...])


def _ffn_chunk(ff):
    for n in range(1, ff // LANES + 1):
        if ff % n == 0 and (ff // n) % LANES == 0 and ff // n <= 1536:
            return ff // n
    return LANES


def _ffn(h, w1, w3, w2, sched, rt, residual=None):
    R, D = h.shape
    E, _, FF = w1.shape
    fc = _ffn_chunk(FF)
    expert, tile, lo, hi, first, count = sched
    hmap = lambda i, c, te, tb, *_: (tb[i], 0)
    in_specs = [pl.BlockSpec((rt, D), hmap),
                pl.BlockSpec((1, D, fc), lambda i, c, te, *_: (te[i], 0, c)),
                pl.BlockSpec((1, D, fc), lambda i, c, te, *_: (te[i], 0, c)),
                pl.BlockSpec((1, fc, D), lambda i, c, te, *_: (te[i], c, 0))]
    ins = [h, w1, w3, w2]
    if residual is not None:
        in_specs.append(pl.BlockSpec((rt, D), hmap))
        ins.append(residual)

    def kern(te_ref, tb_ref, *refs):
        _ffn_kernel(*refs, residual=residual is not None)

    return pl.pallas_call(
        kern,
        grid_spec=pltpu.PrefetchScalarGridSpec(
            num_scalar_prefetch=6,
            grid=(expert.shape[0], FF // fc),
            in_specs=in_specs,
            out_specs=pl.BlockSpec((rt, D), hmap),
            scratch_shapes=[pltpu.VMEM((rt, D), F32)],
        ),
        out_shape=jax.ShapeDtypeStruct((R, D), F32),
        compiler_params=_params(("arbitrary", "arbitrary")),
        name="ffn",
    )(expert, tile, lo, hi, first, count, *ins)


def _dense_schedule(n_tiles, rt):
    i32 = jnp.int32
    return (jnp.zeros((n_tiles,), i32), jnp.arange(n_tiles, dtype=i32), jnp.zeros((n_tiles,), i32),
            jnp.full((n_tiles,), rt, i32), jnp.ones((n_tiles,), i32), jnp.full((1,), n_tiles, i32))


def _expert_schedule(cnt, n_rows, rt):
    i32 = jnp.int32
    E = cnt.shape[0]
    end = jnp.cumsum(cnt)
    off = end - cnt
    t_first = off // rt
    t_last = jnp.maximum(end - 1, off) // rt
    n_vis = jnp.where(cnt > 0, t_last - t_first + 1, 0)
    vis_end = jnp.cumsum(n_vis)
    count = vis_end[-1]
    max_vis = n_rows // rt + E - 1
    p = jnp.minimum(jnp.arange(max_vis, dtype=i32), jnp.maximum(count - 1, 0))
    expert = jnp.minimum(jnp.sum((p[:, None] >= vis_end[None, :]).astype(i32), axis=1), E - 1)
    tile = t_first[expert] + (p - (vis_end[expert] - n_vis[expert]))
    lo = jnp.clip(off[expert] - tile * rt, 0, rt)
    hi = jnp.clip(end[expert] - tile * rt, 0, rt)
    first = jnp.concatenate([jnp.ones((1,), i32), (tile[1:] != tile[:-1]).astype(i32)])
    return (expert, tile.astype(i32), lo.astype(i32), hi.astype(i32), first, count.reshape(1).astype(i32)), off


def _band_proj_kernel(x_ref, gkv_ref, gq_ref, wk_ref, wv_ref, wq_ref, gkn_ref, gqn_ref, bd_ref,
                      k_ref, v_ref, q_ref):
    x = x_ref[...]
    hk = _rms_rows(x, gkv_ref[...]).astype(BF16)
    hq = _rms_rows(x, gq_ref[...]).astype(BF16)
    v_ref[...] = _dot(hk, wv_ref[...]).astype(BF16)
    bd = bd_ref[...]
    inv_d = 1.0 / CA_DIM
    for p in range(k_ref.shape[1] // MXU):
        cols = slice(p * MXU, (p + 1) * MXU)
        kk = _dot(hk, wk_ref[:, cols])
        ssq = _dot((kk * kk).astype(BF16), bd)
        k_ref[:, cols] = (kk * lax.rsqrt(ssq * inv_d + EPS) * gkn_ref[...]).astype(BF16)
        qq = _dot(hq, wq_ref[:, cols])
        ssq = _dot((qq * qq).astype(BF16), bd)
        q_ref[:, cols] = (qq * lax.rsqrt(ssq * inv_d + EPS) * gqn_ref[...]).astype(BF16)


def _band_proj(x2d, seq, g_kv, g_attn, w_k, w_v, w_q, g_kn, g_qn):
    T, D = x2d.shape
    HD = CA_HEADS * CA_DIM
    tm = min(512, seq)
    r = np.arange(MXU)
    bd = jnp.asarray((r[:, None] // CA_DIM == r[None, :] // CA_DIM).astype(np.float32), BF16)
    gkn = jnp.tile(g_kn, MXU // CA_DIM)[None]
    gqn = jnp.tile(g_qn, MXU // CA_DIM)[None] * (CA_DIM ** -0.5 * LOG2E)
    fixed = lambda i: (0, 0)
    row = lambda i: (i, 0)
    return pl.pallas_call(
        _band_proj_kernel,
        grid=(T // tm,),
        in_specs=[pl.BlockSpec((tm, D), row), pl.BlockSpec((1, D), fixed), pl.BlockSpec((1, D), fixed),
                  pl.BlockSpec((D, HD), fixed), pl.BlockSpec((D, HD), fixed), pl.BlockSpec((D, HD), fixed),
                  pl.BlockSpec((1, MXU), fixed), pl.BlockSpec((1, MXU), fixed), pl.BlockSpec((MXU, MXU), fixed)],
        out_specs=[pl.BlockSpec((tm, HD), row)] * 3,
        out_shape=[jax.ShapeDtypeStruct((T, HD), BF16)] * 3,
        compiler_params=_params(("arbitrary",)),
        name="band_proj",
    )(x2d, g_kv[None], g_attn[None], w_k.astype(BF16), w_v.astype(BF16), w_q.astype(BF16), gkn, gqn, bd)


def _band_attn_kernel(q_ref, k_ref, v_ref, bias_ref, o_ref, vt_ref, *, tq, pad):
    seq = q_ref.shape[1]
    win = tq + pad
    for j in range(seq // MXU):
        blk = slice(j * MXU, (j + 1) * MXU)
        vt_ref[:, blk] = v_ref[0, blk, :].astype(F32).T.astype(BF16)

    lane = lax.broadcasted_iota(jnp.int32, (tq, LANES), 1)
    for qi in range(seq // tq):
        rows = slice(qi * tq, (qi + 1) * tq)
        keys = slice(max(0, qi * tq - pad), (qi + 1) * tq)
        nkeys = keys.stop - keys.start
        kwin = k_ref[0, keys, :]
        q2 = q_ref[0, rows, :]
        for hh in range(2):
            vs = slice(hh * CA_DIM, (hh + 1) * CA_DIM)
            qh = jnp.where((lane >= vs.start) & (lane < vs.stop), q2, jnp.zeros_like(q2))
            s = _dot_nt(kwin, qh) + bias_ref[hh, win - nkeys:, :]
            m = jnp.max(s, axis=0, keepdims=True)
            p = jnp.exp2(s - m)
            l = jnp.sum(p, axis=0, keepdims=True)
            acc = _dot(vt_ref[vs, keys], p.astype(BF16))
            o_ref[0, rows, vs] = (acc * (1.0 / l)).T.astype(BF16)


def _band_bias(rel_bias, tq):
    pad = LEFT_CHUNKS * CHUNK
    win = tq + pad
    rel = pad - (np.arange(win + tq - 1) - (tq - 1))
    table = rel_bias[:, np.clip(rel, -REL_CLIP, REL_CLIP) + REL_CLIP].astype(F32) * LOG2E
    cols = jax.vmap(lambda c: lax.dynamic_slice_in_dim(table, tq - 1 - c, win, axis=1))(jnp.arange(tq))
    r = np.arange(win)[:, None]
    c = np.arange(tq)[None, :]
    qc = (c + pad) // CHUNK
    kc = r // CHUNK
    vis = (kc <= qc) & (kc >= qc - LEFT_CHUNKS)
    return jnp.where(jnp.asarray(vis)[None], jnp.transpose(cols, (1, 2, 0)), NEG)


def _band_attn(q, k, v, rel_bias, batch, seq):
    H = CA_HEADS
    HD = H * CA_DIM
    pad = LEFT_CHUNKS * CHUNK
    tq = LANES
    pair = lambda h, b: (b, 0, h)
    shape3 = (batch, seq, HD)
    out = pl.pallas_call(
        functools.partial(_band_attn_kernel, tq=tq, pad=pad),
        grid=(H // 2, batch),
        in_specs=[pl.BlockSpec((1, seq, LANES), pair), pl.BlockSpec((1, seq, LANES), pair),
                  pl.BlockSpec((1, seq, LANES), pair),
                  pl.BlockSpec((2, tq + pad, tq), lambda h, b: (h, 0, 0))],
        out_specs=pl.BlockSpec((1, seq, LANES), pair),
        out_shape=jax.ShapeDtypeStruct(shape3, BF16),
        scratch_shapes=[pltpu.VMEM((LANES, seq), BF16)],
        compiler_params=_params(("arbitrary", "arbitrary")),
        name="band_attn",
    )(q.reshape(shape3), k.reshape(shape3), v.reshape(shape3), _band_bias(rel_bias, tq))
    return out.reshape(batch * seq, HD)


R_E1, R_E2, R_RANK1, R_RANK2, R_G1, R_G2 = 8, 9, 10, 11, 12, 13


def _router_kernel(a_ref, w_ref, x_ref, g_ref, wrh_ref, wrl_ref, xo_ref, ho_ref, r_ref, cnt_ref, run_ref,
                   *, n_experts):
    i = pl.program_id(0)

    @pl.when(i == 0)
    def _():
        run_ref[...] = jnp.zeros_like(run_ref)

    y = x_ref[...] + _dot(a_ref[...], w_ref[...])
    xo_ref[...] = y
    hf = _rms_rows(y, g_ref[...])
    ho_ref[...] = hf
    h_hi = hf.astype(BF16)
    h_lo = (hf - h_hi.astype(F32)).astype(BF16)
    logits = _dot(h_hi, wrh_ref[...]) + (_dot(h_lo, wrh_ref[...]) + _dot(h_hi, wrl_ref[...]))
    tm = logits.shape[0]
    lane = lax.broadcasted_iota(jnp.int32, logits.shape, 1).astype(F32)
    logits = jnp.where(lane < n_experts, logits, NEG)
    m1 = jnp.max(logits, axis=-1, keepdims=True)
    e1 = jnp.min(jnp.where(logits == m1, lane, float(LANES)), axis=-1, keepdims=True)
    rest = jnp.where(lane == e1, NEG, logits)
    m2 = jnp.max(rest, axis=-1, keepdims=True)
    e2 = jnp.min(jnp.where(rest == m2, lane, float(LANES)), axis=-1, keepdims=True)
    ex = jnp.exp(m2 - m1)
    g1 = 1.0 / (1.0 + ex)
    g2 = ex * g1
    oh1 = (lane == e1).astype(F32)
    oh2 = (lane == e2).astype(F32)
    oh = oh1 + oh2
    tri = (lax.broadcasted_iota(jnp.int32, (tm, tm), 0) > lax.broadcasted_iota(jnp.int32, (tm, tm), 1))
    before = _dot(tri.astype(BF16), oh.astype(BF16)) + run_ref[...]
    rank1 = jnp.sum(oh1 * before, axis=-1, keepdims=True)
    rank2 = jnp.sum(oh2 * before, axis=-1, keepdims=True)
    run_ref[...] += jnp.sum(oh, axis=0, keepdims=True)
    rec = oh1 * g1 + oh2 * g2
    for ln, val in ((R_E1, e1), (R_E2, e2), (R_RANK1, rank1), (R_RANK2, rank2), (R_G1, g1), (R_G2, g2)):
        rec = jnp.where(lane == ln, val, rec)
    r_ref[...] = rec
    cnt_ref[...] = run_ref[...]


def _router(a, w, x2d, g, w_router, seq):
    T, D = x2d.shape
    E = w_router.shape[1]
    tm = min(512, seq)
    wr = jnp.zeros((D, LANES), F32).at[:, :E].set(w_router)
    wr_hi = wr.astype(BF16)
    wr_lo = (wr - wr_hi.astype(F32)).astype(BF16)
    row = lambda i: (i, 0)
    fixed = lambda i: (0, 0)
    return pl.pallas_call(
        functools.partial(_router_kernel, n_experts=E),
        grid=(T // tm,),
        in_specs=[pl.BlockSpec((tm, a.shape[1]), row), pl.BlockSpec(w.shape, fixed),
                  pl.BlockSpec((tm, D), row), pl.BlockSpec((1, D), fixed),
                  pl.BlockSpec((D, LANES), fixed), pl.BlockSpec((D, LANES), fixed)],
        out_specs=[pl.BlockSpec((tm, D), row), pl.BlockSpec((tm, D), row),
                   pl.BlockSpec((tm, LANES), row), pl.BlockSpec((1, LANES), fixed)],
        out_shape=[jax.ShapeDtypeStruct((T, D), F32), jax.ShapeDtypeStruct((T, D), F32),
                   jax.ShapeDtypeStruct((T, LANES), F32), jax.ShapeDtypeStruct((1, LANES), F32)],
        scratch_shapes=[pltpu.VMEM((1, LANES), F32)],
        compiler_params=_params(("arbitrary",)),
        name="router",
    )(a, w.astype(BF16), x2d, g[None], wr_hi, wr_lo)


def _scatter_rows_kernel(dest_ref, h_ref, o_ref, idx_ref, isem, sem, *, tm):
    i = pl.program_id(0)
    icopy = pltpu.make_async_copy(dest_ref.at[pl.ds(i * (TOP_K * tm), TOP_K * tm)], idx_ref, isem)
    icopy.start()
    icopy.wait()

    def row_copy(r, d):
        return pltpu.make_async_copy(h_ref.at[pl.ds(r, 1)], o_ref.at[pl.ds(d, 1)], sem)

    def issue(r, _):
        row_copy(r, idx_ref[r]).start(priority=0)
        row_copy(r, idx_ref[tm + r]).start(priority=1)
        return 0

    lax.fori_loop(0, tm, issue, 0)

    def drain(r, _):
        row_copy(0, 0).wait()
        row_copy(0, 0).wait()
        return 0

    lax.fori_loop(0, tm, drain, 0)


def _scatter_rows(h, dest, rows_out, tm):
    T, D = h.shape
    return pl.pallas_call(
        functools.partial(_scatter_rows_kernel, tm=tm),
        grid=(T // tm,),
        in_specs=[pl.BlockSpec(memory_space=pl.ANY), pl.BlockSpec((tm, D), lambda i: (i, 0))],
        out_specs=pl.BlockSpec(memory_space=pl.ANY),
        out_shape=jax.ShapeDtypeStruct((rows_out, D), h.dtype),
        scratch_shapes=[pltpu.SMEM((TOP_K * tm,), jnp.int32), pltpu.SemaphoreType.DMA, pltpu.SemaphoreType.DMA],
        compiler_params=_params(("arbitrary",)),
        name="moe_scatter",
    )(dest, h)


def _combine_kernel(dest_ref, y_ref, x_ref, r_ref, o_ref, idx_ref, b1_ref, b2_ref, isem, sem, *, tm):
    i = pl.program_id(0)
    icopy = pltpu.make_async_copy(dest_ref.at[pl.ds(i * (TOP_K * tm), TOP_K * tm)], idx_ref, isem)
    icopy.start()
    icopy.wait()

    def row_copy(d, buf, r):
        return pltpu.make_async_copy(y_ref.at[pl.ds(d, 1)], buf.at[pl.ds(r, 1)], sem)

    def issue(r, _):
        row_copy(idx_ref[r], b1_ref, r).start(priority=0)
        row_copy(idx_ref[tm + r], b2_ref, r).start(priority=1)
        return 0

    lax.fori_loop(0, tm, issue, 0)

    def drain(r, _):
        row_copy(0, b1_ref, 0).wait()
        row_copy(0, b2_ref, 0).wait()
        return 0

    lax.fori_loop(0, tm, drain, 0)
    rec = r_ref[...]
    lane = lax.broadcasted_iota(jnp.int32, rec.shape, 1)
    g1 = jnp.sum(jnp.where(lane == R_G1, rec, 0.0), axis=-1, keepdims=True)
    g2 = jnp.sum(jnp.where(lane == R_G2, rec, 0.0), axis=-1, keepdims=True)
    o_ref[...] = x_ref[...] + (g1 * b1_ref[...] + g2 * b2_ref[...])


def _combine(y_sorted, dest, x2d, rec, tm):
    T, D = x2d.shape
    row = lambda i: (i, 0)
    return pl.pallas_call(
        functools.partial(_combine_kernel, tm=tm),
        grid=(T // tm,),
        in_specs=[pl.BlockSpec(memory_space=pl.ANY), pl.BlockSpec(memory_space=pl.ANY),
                  pl.BlockSpec((tm, D), row), pl.BlockSpec((tm, LANES), row)],
        out_specs=pl.BlockSpec((tm, D), row),
        out_shape=jax.ShapeDtypeStruct((T, D), F32),
        scratch_shapes=[pltpu.SMEM((TOP_K * tm,), jnp.int32), pltpu.VMEM((tm, D), F32), pltpu.VMEM((tm, D), F32),
                        pltpu.SemaphoreType.DMA, pltpu.SemaphoreType.DMA],
        compiler_params=_params(("arbitrary",)),
        name="moe_combine",
    )(dest, y_sorted, x2d, rec)


def _moe(h, x2d, rec, counts, w1, w3, w2, seq):
    T, D = h.shape
    E = w1.shape[0]
    rt = min(1024, T)
    tm = min(512, seq)
    cnt = counts[0, :E].astype(jnp.int32)
    sched, group_start = _expert_schedule(cnt, TOP_K * T, rt)
    e1 = rec[:, R_E1].astype(jnp.int32)
    e2 = rec[:, R_E2].astype(jnp.int32)
    d1 = group_start[e1] + rec[:, R_RANK1].astype(jnp.int32)
    d2 = group_start[e2] + rec[:, R_RANK2].astype(jnp.int32)
    dest = jnp.concatenate([d1.reshape(T // tm, tm), d2.reshape(T // tm, tm)], axis=1).reshape(-1)

    h_sorted = _scatter_rows(h, dest, TOP_K * T, tm)
    y_sorted = _ffn(h_sorted, w1, w3, w2, sched, rt)
    return _combine(y_sorted, dest, x2d, rec, tm)


def kernel(x, attn_norm, ffn_norm, a_w_dq, a_g_cq, a_w_uq, a_w_dkv, a_g_ckv, a_w_ukv, a_g_qn, a_g_kn, a_w_o,
           kv_norm, kv_w_k, kv_w_v, kv_g_kn, b_w_q, b_g_qn, b_rel_bias, b_w_o, ffn_w1, ffn_w3, ffn_w2,
           moe_router, moe_w1, moe_w3, moe_w2):
    B, S, D = x.shape
    T = B * S
    assert S % 256 == 0 and T % 512 == 0 and (TOP_K * T) % min(1024, T) == 0
    x0 = x.reshape(T, D)

    q, k, v = _mla_proj(x0, S, attn_norm[0], a_w_dq[0], a_g_cq[0], a_w_uq[0], a_w_dkv[0], a_g_ckv[0],
                        a_w_ukv[0], a_g_qn[0], a_g_kn[0])
    o = _mla_attn(q, k, v, B, S)
    x1, h1 = _proj_res_norm(o, a_w_o[0], x0, ffn_norm[0], S)
    rt = min(512, T)
    x2 = _ffn(h1, ffn_w1.astype(BF16), ffn_w3.astype(BF16), ffn_w2.astype(BF16),
              _dense_schedule(T // rt, rt), rt, residual=x1)

    kb, vb, qb = _band_proj(x2, S, kv_norm, attn_norm[1], kv_w_k, kv_w_v, b_w_q[0], kv_g_kn, b_g_qn[0])
    ob = _band_attn(qb, kb, vb, b_rel_bias[0], B, S)
    x3, h3, rec, counts = _router(ob, b_w_o[0], x2, ffn_norm[1], moe_router[0], S)
    x4 = _moe(h3, x3, rec, counts, moe_w1[0].astype(BF16), moe_w3[0].astype(BF16), moe_w2[0].astype(BF16), S)
    return x4.reshape(B, S, D)
```

```python
import functools

import jax
import jax.numpy as jnp
import numpy as np
from jax import lax
from jax.experimental import pallas as pl
from jax.experimental.pallas import tpu as pltpu

F32 = jnp.float32
BF16 = jnp.bfloat16

EPS = 1e-6
NEG = -1e30
ROPE_THETA = 10000.0
LOG2E = 1.4426950408889634

LANES = 128
MXU = 256
VMEM_LIMIT = 52 * 1024 * 1024

CHUNK = 64
LEFT_CHUNKS = 8
REL_CLIP = 128
MLA_HEADS = 16
MLA_NOPE = 64
MLA_ROPE = 32
MLA_QK = MLA_NOPE + MLA_ROPE
MLA_V = 64
CA_HEADS = 16
CA_DIM = 64
TOP_K = 2


def _params(sem):
    return pltpu.CompilerParams(dimension_semantics=sem, vmem_limit_bytes=VMEM_LIMIT)


def _rms_rows(x, g):
    ms = jnp.mean(x * x, axis=-1, keepdims=True)
    return x * lax.rsqrt(ms + EPS) * g


def _dot(a, b):
    return jnp.dot(a, b, preferred_element_type=F32)


def _dot_nt(a, b):
    return lax.dot_general(a, b, (((1,), (1,)), ((), ())), preferred_element_type=F32)


def _mla_proj_kernel(x_ref, gn_ref, wdq_ref, gcq_ref, wuq_ref, wdkv_ref, gckv_ref, wuk_ref, wuv_ref,
                     gq_ref, gkn_ref, ga_ref, gb_ref, tabq_ref, cosk_ref, sink_ref, bd_ref,
                     q_ref, k_ref, v_ref):
    kv_lora = gckv_ref.shape[1]
    h = _rms_rows(x_ref[...], gn_ref[...]).astype(BF16)
    cq = _rms_rows(_dot(h, wdq_ref[...]), gcq_ref[...]).astype(BF16)
    kva = _dot(h, wdkv_ref[...])
    ckv = _rms_rows(kva[:, :kv_lora], gckv_ref[...]).astype(BF16)
    slab_a = kva[:, kv_lora:kv_lora + LANES]
    slab_b = kva[:, kv_lora + LANES:kv_lora + 2 * LANES]
    lane = lax.broadcasted_iota(jnp.int32, slab_a.shape, 1)
    rope_lanes = (lane >= MLA_NOPE) & (lane < MLA_QK)
    kpe_ssq = jnp.sum(jnp.where(rope_lanes, slab_a * slab_a, 0.0), axis=-1, keepdims=True)
    k_rot = slab_a * (ga_ref[...] * cosk_ref[...]) + slab_b * (gb_ref[...] * sink_ref[...])
    v_ref[...] = _dot(ckv, wuv_ref[...]).astype(BF16)

    gq = gq_ref[...] * tabq_ref[...]
    gq2 = jnp.concatenate([gq, gq], axis=1)
    k_rot2 = jnp.concatenate([k_rot, k_rot], axis=1)
    gkn2 = jnp.concatenate([gkn_ref[...], gkn_ref[...]], axis=1)
    bd = bd_ref[...]
    inv_d = 1.0 / MLA_QK
    for p in range(q_ref.shape[1] // MXU):
        cols = slice(p * MXU, (p + 1) * MXU)
        qq = _dot(cq, wuq_ref[:, cols])
        ssq = _dot((qq * qq).astype(BF16), bd)
        q_ref[:, cols] = (qq * lax.rsqrt(ssq * inv_d + EPS) * gq2).astype(BF16)
        kk = _dot(ckv, wuk_ref[:, cols])
        ssq = _dot((kk * kk).astype(BF16), bd) + kpe_ssq
        k_ref[:, cols] = ((kk * gkn2 + k_rot2) * lax.rsqrt(ssq * inv_d + EPS)).astype(BF16)


def _mla_proj(x2d, seq, g_attn, w_dq, g_cq, w_uq, w_dkv, g_ckv, w_ukv, g_qn, g_kn):
    T, D = x2d.shape
    H, NOPE, ROPE, QK, VD = MLA_HEADS, MLA_NOPE, MLA_ROPE, MLA_QK, MLA_V
    half = ROPE // 2
    q_lora = w_dq.shape[1]
    kv_lora = g_ckv.shape[0]
    tm = min(512, seq)

    wq = w_uq.reshape(q_lora, H, QK)
    wq_n, wq_1, wq_2 = wq[..., :NOPE], wq[..., NOPE:NOPE + half], wq[..., NOPE + half:]
    wuq_p = jnp.concatenate([wq_n, wq_1, wq_2, wq_2, wq_1], axis=-1).reshape(q_lora, H * LANES).astype(BF16)
    wkv = w_ukv.reshape(kv_lora, H, NOPE + VD)
    wuk_p = jnp.concatenate([wkv[..., :NOPE], jnp.zeros((kv_lora, H, LANES - NOPE), F32)], axis=-1)
    wuk_p = wuk_p.reshape(kv_lora, H * LANES).astype(BF16)
    wuv_p = wkv[..., NOPE:].reshape(kv_lora, H * VD).astype(BF16)
    pe1, pe2 = w_dkv[:, kv_lora:kv_lora + half], w_dkv[:, kv_lora + half:]
    z = jnp.zeros((D, NOPE), F32)
    wdkv_p = jnp.concatenate([w_dkv[:, :kv_lora], z, pe1, pe2, pe1, pe2, z, pe2, pe1, pe2, pe1], axis=1).astype(BF16)

    gq = jnp.concatenate([g_qn[:NOPE], g_qn[NOPE:NOPE + half], g_qn[NOPE + half:],
                          g_qn[NOPE + half:], g_qn[NOPE:NOPE + half]])[None]
    k1, k2 = g_kn[NOPE:NOPE + half], g_kn[NOPE + half:]
    zn = jnp.zeros((NOPE,), F32)
    gkn = jnp.concatenate([g_kn[:NOPE], zn])[None]
    ga = jnp.concatenate([zn, k1, k2, k1, k2])[None]
    gb = jnp.concatenate([zn, k2, k1, k2, k1])[None]

    inv = ROPE_THETA ** (-jnp.arange(half, dtype=F32) / half)
    ang = jnp.arange(seq, dtype=F32)[:, None] * inv[None, :]
    c, s = jnp.cos(ang), jnp.sin(ang)
    ones = jnp.ones((seq, NOPE), F32)
    zs = jnp.zeros((seq, NOPE), F32)
    tabq = jnp.concatenate([ones, c, c, -s, s], axis=1) * (QK ** -0.5 * LOG2E)
    cosk = jnp.concatenate([zs, c, c, c, c], axis=1)
    sink = jnp.concatenate([zs, -s, s, -s, s], axis=1)

    r = np.arange(MXU)
    bd = ((r[:, None] // LANES == r[None, :] // LANES) & (r[:, None] % LANES < QK)).astype(np.float32)
    bd = jnp.asarray(bd, BF16)

    nseq = seq // tm
    row = lambda i: (i, 0)
    fixed = lambda i: (0, 0)
    pos = lambda i: (i % nseq, 0)
    full = lambda a: pl.BlockSpec(a.shape, fixed)
    ins = [x2d, g_attn[None], w_dq.astype(BF16), g_cq[None], wuq_p, wdkv_p, g_ckv[None], wuk_p, wuv_p,
           gq, gkn, ga, gb, tabq, cosk, sink, bd]
    in_specs = [pl.BlockSpec((tm, D), row)] + [full(a) for a in ins[1:13]]
    in_specs += [pl.BlockSpec((tm, LANES), pos)] * 3 + [full(bd)]
    return pl.pallas_call(
        _mla_proj_kernel,
        grid=(T // tm,),
        in_specs=in_specs,
        out_specs=[pl.BlockSpec((tm, H * LANES), row), pl.BlockSpec((tm, H * LANES), row),
                   pl.BlockSpec((tm, H * VD), row)],
        out_shape=[jax.ShapeDtypeStruct((T, H * LANES), BF16), jax.ShapeDtypeStruct((T, H * LANES), BF16),
                   jax.ShapeDtypeStruct((T, H * VD), BF16)],
        compiler_params=_params(("arbitrary",)),
        name="mla_proj",
    )(*ins)


def _mla_attn_kernel(q_ref, k_ref, v_ref, o_ref, vt_ref, *, tq):
    seq = q_ref.shape[1]
    nq = seq // tq
    for j in range(nq):
        blk = slice(j * tq, (j + 1) * tq)
        vt_ref[:, blk] = v_ref[0, blk, :].astype(F32).T.astype(BF16)

    krow = lax.broadcasted_iota(jnp.int32, (tq, tq), 0) // CHUNK
    qcol = lax.broadcasted_iota(jnp.int32, (tq, tq), 1) // CHUNK
    diag_ok = krow <= qcol
    for qi in range(nq):
        rows = slice(qi * tq, (qi + 1) * tq)
        past = slice(0, qi * tq)
        for hh in range(2):
            hs = slice(hh * LANES, (hh + 1) * LANES)
            vs = slice(hh * MLA_V, (hh + 1) * MLA_V)
            q = q_ref[0, rows, hs]
            sd = jnp.where(diag_ok, _dot_nt(k_ref[0, rows, hs], q), NEG)
            m = jnp.max(sd, axis=0, keepdims=True)
            if qi:
                so = _dot_nt(k_ref[0, past, hs], q)
                m = jnp.maximum(m, jnp.max(so, axis=0, keepdims=True))
            pd = jnp.exp2(sd - m)
            l = jnp.sum(pd, axis=0, keepdims=True)
            acc = _dot(vt_ref[vs, rows], pd.astype(BF16))
            if qi:
                po = jnp.exp2(so - m)
                l = l + jnp.sum(po, axis=0, keepdims=True)
                acc = acc + _dot(vt_ref[vs, past], po.astype(BF16))
            o_ref[0, rows, vs] = (acc * (1.0 / l)).T.astype(BF16)


def _mla_attn(q, k, v, batch, seq):
    H = MLA_HEADS
    q3 = q.reshape(batch, seq, H * LANES)
    k3 = k.reshape(batch, seq, H * LANES)
    v3 = v.reshape(batch, seq, H * MLA_V)
    pair = lambda b, h: (b, 0, h)
    out = pl.pallas_call(
        functools.partial(_mla_attn_kernel, tq=min(256, seq)),
        grid=(batch, H // 2),
        in_specs=[pl.BlockSpec((1, seq, 2 * LANES), pair), pl.BlockSpec((1, seq, 2 * LANES), pair),
                  pl.BlockSpec((1, seq, 2 * MLA_V), pair)],
        out_specs=pl.BlockSpec((1, seq, 2 * MLA_V), pair),
        out_shape=jax.ShapeDtypeStruct((batch, seq, H * MLA_V), BF16),
        scratch_shapes=[pltpu.VMEM((2 * MLA_V, seq), BF16)],
        compiler_params=_params(("arbitrary", "arbitrary")),
        name="mla_attn",
    )(q3, k3, v3)
    return out.reshape(batch * seq, H * MLA_V)


def _proj_res_norm_kernel(a_ref, w_ref, x_ref, g_ref, xo_ref, ho_ref):
    y = x_ref[...] + _dot(a_ref[...], w_ref[...])
    xo_ref[...] = y
    ho_ref[...] = _rms_rows(y, g_ref[...]).astype(BF16)


def _proj_res_norm(a, w, x2d, g, seq):
    T, D = x2d.shape
    tm = min(512, seq)
    row = lambda i: (i, 0)
    fixed = lambda i: (0, 0)
    return pl.pallas_call(
        _proj_res_norm_kernel,
        grid=(T // tm,),
        in_specs=[pl.BlockSpec((tm, a.shape[1]), row), pl.BlockSpec(w.shape, fixed),
                  pl.BlockSpec((tm, D), row), pl.BlockSpec((1, D), fixed)],
        out_specs=[pl.BlockSpec((tm, D), row), pl.BlockSpec((tm, D), row)],
        out_shape=[jax.ShapeDtypeStruct((T, D), F32), jax.ShapeDtypeStruct((T, D), BF16)],
        compiler_params=_params(("arbitrary",)),
        name="proj_res_norm",
    )(a, w.astype(BF16), x2d, g[None])


def _pack_bf16_pairs(x):
    half = x.shape[1] // 2
    bits = lax.bitcast_convert_type(x.astype(BF16).astype(F32), jnp.uint32)
    return bits[:, :half] | (bits[:, half:] >> 16)


def _unpack_bf16_pairs(w):
    hi = lax.bitcast_convert_type(w & jnp.uint32(0xFFFF0000), F32)
    lo = lax.bitcast_convert_type(w << 16, F32)
    return jnp.concatenate([hi, lo], axis=1)


def _ffn_kernel(lo_ref, hi_ref, first_ref, np_ref, h_ref, w1_ref, w3_ref, w2_ref, *rest, residual):
    if residual:
        res_ref, o_ref, acc_ref = rest
    else:
        o_ref, acc_ref = rest
    i, c = pl.program_id(0), pl.program_id(1)

    @pl.when(i < np_ref[0])
    def _():
        @pl.when(c == 0)
        def _():
            acc_ref[...] = jnp.zeros_like(acc_ref)

        h = h_ref[...] if residual else _unpack_bf16_pairs(h_ref[...]).astype(BF16)
        u = _dot(h, w1_ref[0])
        g = _dot(h, w3_ref[0])
        a = (u * jax.nn.sigmoid(u) * g).astype(BF16)
        acc_ref[...] += _dot(a, w2_ref[0])

        last = c == pl.num_programs(1) - 1

        def result():
            return res_ref[...] + acc_ref[...] if residual else _pack_bf16_pairs(acc_ref[...])

        @pl.when(last & (first_ref[i] == 1))
        def _():
            o_ref[...] = result()

        @pl.when(last & (first_ref[i] == 0))
        def _():
            rows = lax.broadcasted_iota(jnp.int32, o_ref.shape, 0)
            mine = (rows >= lo_ref[i]) & (rows < hi_ref[i])
            o_ref[...] = jnp.where(mine, result(), o_ref[...])


def _ffn(h, w1, w3, w2, sched, rt, fc, residual=None):
    R, Dh = h.shape
    E, D, FF = w1.shape
    assert FF % fc == 0 and fc % MXU == 0
    expert, tile, lo, hi, first, count = sched
    hmap = lambda i, c, te, tb, *_: (tb[i], 0)
    mode = dict(pipeline_mode=pl.Buffered(1)) if (FF == fc and E == 1) else {}
    in_specs = [pl.BlockSpec((rt, Dh), hmap),
                pl.BlockSpec((1, D, fc), lambda i, c, te, *_: (te[i], 0, c), **mode),
                pl.BlockSpec((1, D, fc), lambda i, c, te, *_: (te[i], 0, c), **mode),
                pl.BlockSpec((1, fc, D), lambda i, c, te, *_: (te[i], c, 0), **mode)]
    ins = [h, w1, w3, w2]
    if residual is not None:
        in_specs.append(pl.BlockSpec((rt, D), hmap))
        ins.append(residual)

    def kern(te_ref, tb_ref, *refs):
        _ffn_kernel(*refs, residual=residual is not None)

    return pl.pallas_call(
        kern,
        grid_spec=pltpu.PrefetchScalarGridSpec(
            num_scalar_prefetch=6,
            grid=(expert.shape[0], FF // fc),
            in_specs=in_specs,
            out_specs=pl.BlockSpec((rt, D if residual is not None else Dh), hmap),
            scratch_shapes=[pltpu.VMEM((rt, D), F32)],
        ),
        out_shape=jax.ShapeDtypeStruct((R, D), F32) if residual is not None
        else jax.ShapeDtypeStruct((R, Dh), jnp.uint32),
        compiler_params=_params(("arbitrary", "arbitrary")),
        name="ffn",
    )(expert, tile, lo, hi, first, count, *ins)


def _dense_schedule(n_tiles, rt):
    i32 = jnp.int32
    return (jnp.zeros((n_tiles,), i32), jnp.arange(n_tiles, dtype=i32), jnp.zeros((n_tiles,), i32),
            jnp.full((n_tiles,), rt, i32), jnp.ones((n_tiles,), i32), jnp.full((1,), n_tiles, i32))


def _expert_schedule(cnt, n_rows, rt):
    i32 = jnp.int32
    E = cnt.shape[0]
    end = jnp.cumsum(cnt)
    off = end - cnt
    t_first = off // rt
    t_last = jnp.maximum(end - 1, off) // rt
    n_vis = jnp.where(cnt > 0, t_last - t_first + 1, 0)
    vis_end = jnp.cumsum(n_vis)
    count = vis_end[-1]
    max_vis = n_rows // rt + E - 1
    p = jnp.minimum(jnp.arange(max_vis, dtype=i32), jnp.maximum(count - 1, 0))
    expert = jnp.minimum(jnp.sum((p[:, None] >= vis_end[None, :]).astype(i32), axis=1), E - 1)
    tile = t_first[expert] + (p - (vis_end[expert] - n_vis[expert]))
    lo = jnp.clip(off[expert] - tile * rt, 0, rt)
    hi = jnp.clip(end[expert] - tile * rt, 0, rt)
    first = jnp.concatenate([jnp.ones((1,), i32), (tile[1:] != tile[:-1]).astype(i32)])
    return (expert, tile.astype(i32), lo.astype(i32), hi.astype(i32), first, count.reshape(1).astype(i32)), off


def _band_proj_kernel(x_ref, gkv_ref, gq_ref, wk_ref, wv_ref, wq_ref, gkn_ref, gqn_ref, bd_ref,
                      k_ref, v_ref, q_ref):
    x = x_ref[...]
    hk = _rms_rows(x, gkv_ref[...]).astype(BF16)
    hq = _rms_rows(x, gq_ref[...]).astype(BF16)
    v_ref[...] = _dot(hk, wv_ref[...]).astype(BF16)
    bd = bd_ref[...]
    inv_d = 1.0 / CA_DIM
    for p in range(k_ref.shape[1] // MXU):
        cols = slice(p * MXU, (p + 1) * MXU)
        kk = _dot(hk, wk_ref[:, cols])
        ssq = _dot((kk * kk).astype(BF16), bd)
        k_ref[:, cols] = (kk * lax.rsqrt(ssq * inv_d + EPS) * gkn_ref[...]).astype(BF16)
        qq = _dot(hq, wq_ref[:, cols])
        ssq = _dot((qq * qq).astype(BF16), bd)
        q_ref[:, cols] = (qq * lax.rsqrt(ssq * inv_d + EPS) * gqn_ref[...]).astype(BF16)


def _band_proj(x2d, seq, g_kv, g_attn, w_k, w_v, w_q, g_kn, g_qn):
    T, D = x2d.shape
    HD = CA_HEADS * CA_DIM
    tm = min(512, seq)
    r = np.arange(MXU)
    bd = jnp.asarray((r[:, None] // CA_DIM == r[None, :] // CA_DIM).astype(np.float32), BF16)
    gkn = jnp.tile(g_kn, MXU // CA_DIM)[None]
    gqn = jnp.tile(g_qn, MXU // CA_DIM)[None] * (CA_DIM ** -0.5 * LOG2E)
    fixed = lambda i: (0, 0)
    row = lambda i: (i, 0)
    return pl.pallas_call(
        _band_proj_kernel,
        grid=(T // tm,),
        in_specs=[pl.BlockSpec((tm, D), row), pl.BlockSpec((1, D), fixed), pl.BlockSpec((1, D), fixed),
                  pl.BlockSpec((D, HD), fixed), pl.BlockSpec((D, HD), fixed), pl.BlockSpec((D, HD), fixed),
                  pl.BlockSpec((1, MXU), fixed), pl.BlockSpec((1, MXU), fixed), pl.BlockSpec((MXU, MXU), fixed)],
        out_specs=[pl.BlockSpec((tm, HD), row)] * 3,
        out_shape=[jax.ShapeDtypeStruct((T, HD), BF16)] * 3,
        compiler_params=_params(("arbitrary",)),
        name="band_proj",
    )(x2d, g_kv[None], g_attn[None], w_k.astype(BF16), w_v.astype(BF16), w_q.astype(BF16), gkn, gqn, bd)


def _band_attn_kernel(q_ref, k_ref, v_ref, bias_ref, o_ref, vt_ref, *, tq, pad):
    seq = q_ref.shape[1]
    win = tq + pad
    for j in range(seq // MXU):
        blk = slice(j * MXU, (j + 1) * MXU)
        vt_ref[:, blk] = v_ref[0, blk, :].astype(F32).T.astype(BF16)

    first_head = lax.broadcasted_iota(jnp.int32, (tq, LANES), 1) < CA_DIM
    for qi in range(seq // tq):
        rows = slice(qi * tq, (qi + 1) * tq)
        keys = slice(max(0, qi * tq - pad), (qi + 1) * tq)
        nkeys = keys.stop - keys.start
        q2 = q_ref[0, rows, :]
        zero = jnp.zeros_like(q2)
        qbd = jnp.concatenate([jnp.where(first_head, q2, zero), jnp.where(first_head, zero, q2)], axis=0)
        s = _dot_nt(k_ref[0, keys, :], qbd) + bias_ref[0, win - nkeys:, :]
        m = jnp.max(s, axis=0, keepdims=True)
        p = jnp.exp2(s - m)
        r = 1.0 / jnp.sum(p, axis=0, keepdims=True)
        pb = p.astype(BF16)
        acc0 = _dot(vt_ref[:CA_DIM, keys], pb[:, :tq]) * r[:, :tq]
        acc1 = _dot(vt_ref[CA_DIM:, keys], pb[:, tq:]) * r[:, tq:]
        o_ref[0, rows, :] = jnp.concatenate([acc0, acc1], axis=0).T.astype(BF16)


def _band_bias(rel_bias, tq):
    H = rel_bias.shape[0]
    pad = LEFT_CHUNKS * CHUNK
    win = tq + pad
    n = win + tq - 1
    rel = pad - (np.arange(n) - (tq - 1))
    table = rel_bias[:, np.clip(rel, -REL_CLIP, REL_CLIP) + REL_CLIP].astype(F32) * LOG2E
    period = jnp.concatenate([table, jnp.zeros((H, 1), F32)], axis=1)
    skew = jnp.tile(period, (1, tq))[:, :tq * n].reshape(H, tq, n)
    bias = jnp.transpose(skew[:, :, tq - 1:tq - 1 + win], (0, 2, 1))
    r = np.arange(win)[:, None]
    c = np.arange(tq)[None, :]
    qc = (c + pad) // CHUNK
    kc = r // CHUNK
    vis = (kc <= qc) & (kc >= qc - LEFT_CHUNKS)
    bias = jnp.where(jnp.asarray(vis)[None], bias, NEG)
    return bias.reshape(H // 2, 2, win, tq).transpose(0, 2, 1, 3).reshape(H // 2, win, 2 * tq)


def _band_attn(q, k, v, rel_bias, batch, seq):
    H = CA_HEADS
    HD = H * CA_DIM
    pad = LEFT_CHUNKS * CHUNK
    tq = LANES
    pair = lambda h, b: (b, 0, h)
    shape3 = (batch, seq, HD)
    out = pl.pallas_call(
        functools.partial(_band_attn_kernel, tq=tq, pad=pad),
        grid=(H // 2, batch),
        in_specs=[pl.BlockSpec((1, seq, LANES), pair), pl.BlockSpec((1, seq, LANES), pair),
                  pl.BlockSpec((1, seq, LANES), pair),
                  pl.BlockSpec((1, tq + pad, 2 * tq), lambda h, b: (h, 0, 0))],
        out_specs=pl.BlockSpec((1, seq, LANES), pair),
        out_shape=jax.ShapeDtypeStruct(shape3, BF16),
        scratch_shapes=[pltpu.VMEM((LANES, seq), BF16)],
        compiler_params=_params(("arbitrary", "arbitrary")),
        name="band_attn",
    )(q.reshape(shape3), k.reshape(shape3), v.reshape(shape3), _band_bias(rel_bias, tq))
    return out.reshape(batch * seq, HD)


R_E1, R_E2, R_RANK1, R_RANK2, R_G1, R_G2 = 8, 9, 10, 11, 12, 13


def _router_kernel(a_ref, w_ref, x_ref, g_ref, wrh_ref, wrl_ref, xo_ref, ho_ref, r_ref, cnt_ref, run_ref,
                   *, n_experts):
    i = pl.program_id(0)

    @pl.when(i == 0)
    def _():
        run_ref[...] = jnp.zeros_like(run_ref)

    y = x_ref[...] + _dot(a_ref[...], w_ref[...])
    xo_ref[...] = y
    hf = _rms_rows(y, g_ref[...])
    ho_ref[...] = _pack_bf16_pairs(hf)
    h_hi = hf.astype(BF16)
    h_lo = (hf - h_hi.astype(F32)).astype(BF16)
    logits = _dot(h_hi, wrh_ref[...]) + (_dot(h_lo, wrh_ref[...]) + _dot(h_hi, wrl_ref[...]))
    tm = logits.shape[0]
    lane = lax.broadcasted_iota(jnp.int32, logits.shape, 1).astype(F32)
    logits = jnp.where(lane < n_experts, logits, NEG)
    m1 = jnp.max(logits, axis=-1, keepdims=True)
    e1 = jnp.min(jnp.where(logits == m1, lane, float(LANES)), axis=-1, keepdims=True)
    rest = jnp.where(lane == e1, NEG, logits)
    m2 = jnp.max(rest, axis=-1, keepdims=True)
    e2 = jnp.min(jnp.where(rest == m2, lane, float(LANES)), axis=-1, keepdims=True)
    ex = jnp.exp(m2 - m1)
    g1 = 1.0 / (1.0 + ex)
    g2 = ex * g1
    oh1 = (lane == e1).astype(F32)
    oh2 = (lane == e2).astype(F32)
    oh = oh1 + oh2
    tri = (lax.broadcasted_iota(jnp.int32, (tm, tm), 0) > lax.broadcasted_iota(jnp.int32, (tm, tm), 1))
    before = _dot(tri.astype(BF16), oh.astype(BF16)) + run_ref[...]
    rank1 = jnp.sum(oh1 * before, axis=-1, keepdims=True)
    rank2 = jnp.sum(oh2 * before, axis=-1, keepdims=True)
    run_ref[...] += jnp.sum(oh, axis=0, keepdims=True)
    rec = oh1 * g1 + oh2 * g2
    for ln, val in ((R_E1, e1), (R_E2, e2), (R_RANK1, rank1), (R_RANK2, rank2), (R_G1, g1), (R_G2, g2)):
        rec = jnp.where(lane == ln, val, rec)
    r_ref[...] = rec
    cnt_ref[...] = run_ref[...]


def _router(a, w, x2d, g, w_router, seq):
    T, D = x2d.shape
    E = w_router.shape[1]
    tm = min(512, seq)
    wr = jnp.zeros((D, LANES), F32).at[:, :E].set(w_router)
    wr_hi = wr.astype(BF16)
    wr_lo = (wr - wr_hi.astype(F32)).astype(BF16)
    row = lambda i: (i, 0)
    fixed = lambda i: (0, 0)
    return pl.pallas_call(
        functools.partial(_router_kernel, n_experts=E),
        grid=(T // tm,),
        in_specs=[pl.BlockSpec((tm, a.shape[1]), row), pl.BlockSpec(w.shape, fixed),
                  pl.BlockSpec((tm, D), row), pl.BlockSpec((1, D), fixed),
                  pl.BlockSpec((D, LANES), fixed), pl.BlockSpec((D, LANES), fixed)],
        out_specs=[pl.BlockSpec((tm, D), row), pl.BlockSpec((tm, D // 2), row),
                   pl.BlockSpec((tm, LANES), row), pl.BlockSpec((1, LANES), fixed)],
        out_shape=[jax.ShapeDtypeStruct((T, D), F32), jax.ShapeDtypeStruct((T, D // 2), jnp.uint32),
                   jax.ShapeDtypeStruct((T, LANES), F32), jax.ShapeDtypeStruct((1, LANES), F32)],
        scratch_shapes=[pltpu.VMEM((1, LANES), F32)],
        compiler_params=_params(("arbitrary",)),
        name="router",
    )(a, w.astype(BF16), x2d, g[None], wr_hi, wr_lo)


def _scatter_rows_kernel(dest_ref, h_ref, o_ref, idx_ref, isem, sem, *, tm):
    i = pl.program_id(0)
    icopy = pltpu.make_async_copy(dest_ref.at[pl.ds(i * (TOP_K * tm), TOP_K * tm)], idx_ref, isem)
    icopy.start()
    icopy.wait()

    def row_copy(r, d):
        return pltpu.make_async_copy(h_ref.at[pl.ds(r, 1)], o_ref.at[pl.ds(d, 1)], sem)

    def issue(r, _):
        row_copy(r, idx_ref[r]).start(priority=0)
        row_copy(r, idx_ref[tm + r]).start(priority=1)
        return 0

    lax.fori_loop(0, tm, issue, 0)

    def drain(r, _):
        row_copy(0, 0).wait()
        row_copy(0, 0).wait()
        return 0

    lax.fori_loop(0, tm, drain, 0)


def _scatter_rows(h, dest, rows_out, tm):
    T, D = h.shape
    return pl.pallas_call(
        functools.partial(_scatter_rows_kernel, tm=tm),
        grid=(T // tm,),
        in_specs=[pl.BlockSpec(memory_space=pl.ANY), pl.BlockSpec((tm, D), lambda i: (i, 0))],
        out_specs=pl.BlockSpec(memory_space=pl.ANY),
        out_shape=jax.ShapeDtypeStruct((rows_out, D), h.dtype),
        scratch_shapes=[pltpu.SMEM((TOP_K * tm,), jnp.int32), pltpu.SemaphoreType.DMA, pltpu.SemaphoreType.DMA],
        compiler_params=_params(("arbitrary",)),
        name="moe_scatter",
    )(dest, h)


def _combine_kernel(dest_ref, y_ref, x_ref, r_ref, o_ref, idx_ref, b1_ref, b2_ref, isem, sem, *, tm):
    i = pl.program_id(0)
    icopy = pltpu.make_async_copy(dest_ref.at[pl.ds(i * (TOP_K * tm), TOP_K * tm)], idx_ref, isem)
    icopy.start()
    icopy.wait()

    def row_copy(d, buf, r):
        return pltpu.make_async_copy(y_ref.at[pl.ds(d, 1)], buf.at[pl.ds(r, 1)], sem)

    def issue(r, _):
        row_copy(idx_ref[r], b1_ref, r).start(priority=0)
        row_copy(idx_ref[tm + r], b2_ref, r).start(priority=1)
        return 0

    lax.fori_loop(0, tm, issue, 0)

    def drain(r, _):
        row_copy(0, b1_ref, 0).wait()
        row_copy(0, b2_ref, 0).wait()
        return 0

    lax.fori_loop(0, tm, drain, 0)
    rec = r_ref[...]
    lane = lax.broadcasted_iota(jnp.int32, rec.shape, 1)
    g1 = jnp.sum(jnp.where(lane == R_G1, rec, 0.0), axis=-1, keepdims=True)
    g2 = jnp.sum(jnp.where(lane == R_G2, rec, 0.0), axis=-1, keepdims=True)
    o_ref[...] = x_ref[...] + (g1 * _unpack_bf16_pairs(b1_ref[...]) + g2 * _unpack_bf16_pairs(b2_ref[...]))


def _combine(y_sorted, dest, x2d, rec, tm):
    T, D = x2d.shape
    Dh = y_sorted.shape[1]
    row = lambda i: (i, 0)
    return pl.pallas_call(
        functools.partial(_combine_kernel, tm=tm),
        grid=(T // tm,),
        in_specs=[pl.BlockSpec(memory_space=pl.ANY), pl.BlockSpec(memory_space=pl.ANY),
                  pl.BlockSpec((tm, D), row), pl.BlockSpec((tm, LANES), row)],
        out_specs=pl.BlockSpec((tm, D), row),
        out_shape=jax.ShapeDtypeStruct((T, D), F32),
        scratch_shapes=[pltpu.SMEM((TOP_K * tm,), jnp.int32), pltpu.VMEM((tm, Dh), jnp.uint32),
                        pltpu.VMEM((tm, Dh), jnp.uint32), pltpu.SemaphoreType.DMA, pltpu.SemaphoreType.DMA],
        compiler_params=_params(("arbitrary",)),
        name="moe_combine",
    )(dest, y_sorted, x2d, rec)


def _moe_chunk(ff):
    n = ff // MXU
    return max(d for d in range(1, n + 1) if n % d == 0 and d <= 8) * MXU


def _moe(h, x2d, rec, counts, w1, w3, w2, seq):
    T = h.shape[0]
    E = w1.shape[0]
    rt = min(512, T)
    tm = min(512, seq)
    cnt = counts[0, :E].astype(jnp.int32)
    sched, group_start = _expert_schedule(cnt, TOP_K * T, rt)
    e1 = rec[:, R_E1].astype(jnp.int32)
    e2 = rec[:, R_E2].astype(jnp.int32)
    d1 = group_start[e1] + rec[:, R_RANK1].astype(jnp.int32)
    d2 = group_start[e2] + rec[:, R_RANK2].astype(jnp.int32)
    dest = jnp.concatenate([d1.reshape(T // tm, tm), d2.reshape(T // tm, tm)], axis=1).reshape(-1)

    h_sorted = _scatter_rows(h, dest, TOP_K * T, tm)
    y_sorted = _ffn(h_sorted, w1, w3, w2, sched, rt, _moe_chunk(w1.shape[2]))
    return _combine(y_sorted, dest, x2d, rec, tm)


def kernel(x, attn_norm, ffn_norm, a_w_dq, a_g_cq, a_w_uq, a_w_dkv, a_g_ckv, a_w_ukv, a_g_qn, a_g_kn, a_w_o,
           kv_norm, kv_w_k, kv_w_v, kv_g_kn, b_w_q, b_g_qn, b_rel_bias, b_w_o, ffn_w1, ffn_w3, ffn_w2,
           moe_router, moe_w1, moe_w3, moe_w2):
    B, S, D = x.shape
    T = B * S
    assert S % 256 == 0 and T % 512 == 0
    x0 = x.reshape(T, D)

    q, k, v = _mla_proj(x0, S, attn_norm[0], a_w_dq[0], a_g_cq[0], a_w_uq[0], a_w_dkv[0], a_g_ckv[0],
                        a_w_ukv[0], a_g_qn[0], a_g_kn[0])
    o = _mla_attn(q, k, v, B, S)
    x1, h1 = _proj_res_norm(o, a_w_o[0], x0, ffn_norm[0], S)
    rt = min(512, T)
    x2 = _ffn(h1, ffn_w1.astype(BF16), ffn_w3.astype(BF16), ffn_w2.astype(BF16),
              _dense_schedule(T // rt, rt), rt, ffn_w1.shape[2], residual=x1)

    kb, vb, qb = _band_proj(x2, S, kv_norm, attn_norm[1], kv_w_k, kv_w_v, b_w_q[0], kv_g_kn, b_g_qn[0])
    ob = _band_attn(qb, kb, vb, b_rel_bias[0], B, S)
    x3, h3, rec, counts = _router(ob, b_w_o[0], x2, ffn_norm[1], moe_router[0], S)
    x4 = _moe(h3, x3, rec, counts, moe_w1[0].astype(BF16), moe_w3[0].astype(BF16), moe_w2[0].astype(BF16), S)
    return x4.reshape(B, S, D)
```

```python
import functools

import jax
import jax.numpy as jnp
import numpy as np
from jax import lax
from jax.experimental import pallas as pl
from jax.experimental.pallas import tpu as pltpu

F32 = jnp.float32
BF16 = jnp.bfloat16

EPS = 1e-6
NEG = -1e30
ROPE_THETA = 10000.0
LOG2E = 1.4426950408889634

LANES = 128
MXU = 256
VMEM_LIMIT = 52 * 1024 * 1024

CHUNK = 64
LEFT_CHUNKS = 8
REL_CLIP = 128
MLA_HEADS = 16
MLA_NOPE = 64
MLA_ROPE = 32
MLA_QK = MLA_NOPE + MLA_ROPE
MLA_V = 64
CA_HEADS = 16
CA_DIM = 64
TOP_K = 2


def _params(sem):
    return pltpu.CompilerParams(dimension_semantics=sem, vmem_limit_bytes=VMEM_LIMIT)


def _rms_rows(x, g):
    ms = jnp.mean(x * x, axis=-1, keepdims=True)
    return x * lax.rsqrt(ms + EPS) * g


def _dot(a, b):
    return jnp.dot(a, b, preferred_element_type=F32)


def _dot_nt(a, b):
    return lax.dot_general(a, b, (((1,), (1,)), ((), ())), preferred_element_type=F32)


def _run_ahead(units, produce, consume, ahead):
    pending = [produce(*u) for u in units[:ahead]]
    for n, u in enumerate(units):
        if n + ahead < len(units):
            pending.append(produce(*units[n + ahead]))
        consume(*u, pending.pop(0))


def _mla_proj_kernel(x_ref, gn_ref, wdq_ref, gcq_ref, wuq_ref, wdkv_ref, gckv_ref, wuk_ref, wuv_ref,
                     gq_ref, gkn_ref, ga_ref, gb_ref, tabq_ref, cosk_ref, sink_ref, bd_ref,
                     q_ref, k_ref, v_ref):
    kv_lora = gckv_ref.shape[1]
    h = _rms_rows(x_ref[...], gn_ref[...]).astype(BF16)
    cq = _rms_rows(_dot(h, wdq_ref[...]), gcq_ref[...]).astype(BF16)
    kva = _dot(h, wdkv_ref[...])
    ckv = _rms_rows(kva[:, :kv_lora], gckv_ref[...]).astype(BF16)
    slab_a = kva[:, kv_lora:kv_lora + LANES]
    slab_b = kva[:, kv_lora + LANES:kv_lora + 2 * LANES]
    lane = lax.broadcasted_iota(jnp.int32, slab_a.shape, 1)
    rope_lanes = (lane >= MLA_NOPE) & (lane < MLA_QK)
    kpe_ssq = jnp.sum(jnp.where(rope_lanes, slab_a * slab_a, 0.0), axis=-1, keepdims=True)
    k_rot = slab_a * (ga_ref[...] * cosk_ref[...]) + slab_b * (gb_ref[...] * sink_ref[...])
    v_ref[...] = _dot(ckv, wuv_ref[...]).astype(BF16)

    gq = gq_ref[...] * tabq_ref[...]
    gq2 = jnp.concatenate([gq, gq], axis=1)
    k_rot2 = jnp.concatenate([k_rot, k_rot], axis=1)
    gkn2 = jnp.concatenate([gkn_ref[...], gkn_ref[...]], axis=1)
    bd = bd_ref[...]
    inv_d = 1.0 / MLA_QK

    def project(kind, cols):
        return _dot(cq, wuq_ref[:, cols]) if kind == "q" else _dot(ckv, wuk_ref[:, cols])

    def normalise(kind, cols, y):
        ssq = _dot((y * y).astype(BF16), bd)
        if kind == "q":
            q_ref[:, cols] = (y * lax.rsqrt(ssq * inv_d + EPS) * gq2).astype(BF16)
        else:
            k_ref[:, cols] = ((y * gkn2 + k_rot2) * lax.rsqrt((ssq + kpe_ssq) * inv_d + EPS)).astype(BF16)

    units = [(kind, slice(p * MXU, (p + 1) * MXU)) for p in range(q_ref.shape[1] // MXU) for kind in ("q", "k")]
    _run_ahead(units, project, normalise, 2)


def _mla_proj(x2d, seq, g_attn, w_dq, g_cq, w_uq, w_dkv, g_ckv, w_ukv, g_qn, g_kn):
    T, D = x2d.shape
    H, NOPE, ROPE, QK, VD = MLA_HEADS, MLA_NOPE, MLA_ROPE, MLA_QK, MLA_V
    half = ROPE // 2
    q_lora = w_dq.shape[1]
    kv_lora = g_ckv.shape[0]
    tm = min(512, seq)

    wq = w_uq.reshape(q_lora, H, QK)
    wq_n, wq_1, wq_2 = wq[..., :NOPE], wq[..., NOPE:NOPE + half], wq[..., NOPE + half:]
    wuq_p = jnp.concatenate([wq_n, wq_1, wq_2, wq_2, wq_1], axis=-1).reshape(q_lora, H * LANES).astype(BF16)
    wkv = w_ukv.reshape(kv_lora, H, NOPE + VD)
    wuk_p = jnp.concatenate([wkv[..., :NOPE], jnp.zeros((kv_lora, H, LANES - NOPE), F32)], axis=-1)
    wuk_p = wuk_p.reshape(kv_lora, H * LANES).astype(BF16)
    wuv_p = wkv[..., NOPE:].reshape(kv_lora, H * VD).astype(BF16)
    pe1, pe2 = w_dkv[:, kv_lora:kv_lora + half], w_dkv[:, kv_lora + half:]
    z = jnp.zeros((D, NOPE), F32)
    wdkv_p = jnp.concatenate([w_dkv[:, :kv_lora], z, pe1, pe2, pe1, pe2, z, pe2, pe1, pe2, pe1], axis=1).astype(BF16)

    gq = jnp.concatenate([g_qn[:NOPE], g_qn[NOPE:NOPE + half], g_qn[NOPE + half:],
                          g_qn[NOPE + half:], g_qn[NOPE:NOPE + half]])[None]
    k1, k2 = g_kn[NOPE:NOPE + half], g_kn[NOPE + half:]
    zn = jnp.zeros((NOPE,), F32)
    gkn = jnp.concatenate([g_kn[:NOPE], zn])[None]
    ga = jnp.concatenate([zn, k1, k2, k1, k2])[None]
    gb = jnp.concatenate([zn, k2, k1, k2, k1])[None]

    inv = ROPE_THETA ** (-jnp.arange(half, dtype=F32) / half)
    ang = jnp.arange(seq, dtype=F32)[:, None] * inv[None, :]
    c, s = jnp.cos(ang), jnp.sin(ang)
    ones = jnp.ones((seq, NOPE), F32)
    zs = jnp.zeros((seq, NOPE), F32)
    tabq = jnp.concatenate([ones, c, c, -s, s], axis=1) * (QK ** -0.5 * LOG2E)
    cosk = jnp.concatenate([zs, c, c, c, c], axis=1)
    sink = jnp.concatenate([zs, -s, s, -s, s], axis=1)

    r = np.arange(MXU)
    bd = ((r[:, None] // LANES == r[None, :] // LANES) & (r[:, None] % LANES < QK)).astype(np.float32)
    bd = jnp.asarray(bd, BF16)

    nseq = seq // tm
    row = lambda i: (i, 0)
    fixed = lambda i: (0, 0)
    pos = lambda i: (i % nseq, 0)
    full = lambda a: pl.BlockSpec(a.shape, fixed)
    ins = [x2d, g_attn[None], w_dq.astype(BF16), g_cq[None], wuq_p, wdkv_p, g_ckv[None], wuk_p, wuv_p,
           gq, gkn, ga, gb, tabq, cosk, sink, bd]
    in_specs = [pl.BlockSpec((tm, D), row)] + [full(a) for a in ins[1:13]]
    in_specs += [pl.BlockSpec((tm, LANES), pos)] * 3 + [full(bd)]
    return pl.pallas_call(
        _mla_proj_kernel,
        grid=(T // tm,),
        in_specs=in_specs,
        out_specs=[pl.BlockSpec((tm, H * LANES), row), pl.BlockSpec((tm, H * LANES), row),
                   pl.BlockSpec((tm, H * VD), row)],
        out_shape=[jax.ShapeDtypeStruct((T, H * LANES), BF16), jax.ShapeDtypeStruct((T, H * LANES), BF16),
                   jax.ShapeDtypeStruct((T, H * VD), BF16)],
        compiler_params=_params(("arbitrary",)),
        name="mla_proj",
    )(*ins)


def _mla_attn_kernel(q_ref, k_ref, v_ref, o_ref, vt_ref, *, tq):
    seq = q_ref.shape[1]
    nq = seq // tq
    for j in range(nq):
        blk = slice(j * tq, (j + 1) * tq)
        vt_ref[:, blk] = v_ref[0, blk, :].astype(F32).T.astype(BF16)

    krow = lax.broadcasted_iota(jnp.int32, (tq, tq), 0) // CHUNK
    qcol = lax.broadcasted_iota(jnp.int32, (tq, tq), 1) // CHUNK
    diag_ok = krow <= qcol

    def scores(qi, hh):
        rows = slice(qi * tq, (qi + 1) * tq)
        hs = slice(hh * LANES, (hh + 1) * LANES)
        q = q_ref[0, rows, hs]
        sd = jnp.where(diag_ok, _dot_nt(k_ref[0, rows, hs], q), NEG)
        so = _dot_nt(k_ref[0, :qi * tq, hs], q) if qi else None
        return sd, so

    def finish(qi, hh, s):
        sd, so = s
        rows = slice(qi * tq, (qi + 1) * tq)
        past = slice(0, qi * tq)
        vs = slice(hh * MLA_V, (hh + 1) * MLA_V)
        m = jnp.max(sd, axis=0, keepdims=True)
        if qi:
            m = jnp.maximum(m, jnp.max(so, axis=0, keepdims=True))
        pd = jnp.exp2(sd - m)
        l = jnp.sum(pd, axis=0, keepdims=True)
        acc = _dot(vt_ref[vs, rows], pd.astype(BF16))
        if qi:
            po = jnp.exp2(so - m)
            l = l + jnp.sum(po, axis=0, keepdims=True)
            acc = acc + _dot(vt_ref[vs, past], po.astype(BF16))
        o_ref[0, rows, vs] = (acc * (1.0 / l)).T.astype(BF16)

    _run_ahead([(qi, hh) for qi in range(nq) for hh in range(2)], scores, finish, 3)


def _mla_attn(q, k, v, batch, seq):
    H = MLA_HEADS
    q3 = q.reshape(batch, seq, H * LANES)
    k3 = k.reshape(batch, seq, H * LANES)
    v3 = v.reshape(batch, seq, H * MLA_V)
    pair = lambda b, h: (b, 0, h)
    out = pl.pallas_call(
        functools.partial(_mla_attn_kernel, tq=min(256, seq)),
        grid=(batch, H // 2),
        in_specs=[pl.BlockSpec((1, seq, 2 * LANES), pair), pl.BlockSpec((1, seq, 2 * LANES), pair),
                  pl.BlockSpec((1, seq, 2 * MLA_V), pair)],
        out_specs=pl.BlockSpec((1, seq, 2 * MLA_V), pair),
        out_shape=jax.ShapeDtypeStruct((batch, seq, H * MLA_V), BF16),
        scratch_shapes=[pltpu.VMEM((2 * MLA_V, seq), BF16)],
        compiler_params=_params(("arbitrary", "arbitrary")),
        name="mla_attn",
    )(q3, k3, v3)
    return out.reshape(batch * seq, H * MLA_V)


def _proj_res_norm_kernel(a_ref, w_ref, x_ref, g_ref, xo_ref, ho_ref):
    y = x_ref[...] + _dot(a_ref[...], w_ref[...])
    xo_ref[...] = y
    ho_ref[...] = _rms_rows(y, g_ref[...]).astype(BF16)


def _proj_res_norm(a, w, x2d, g, seq):
    T, D = x2d.shape
    tm = min(512, seq)
    row = lambda i: (i, 0)
    fixed = lambda i: (0, 0)
    return pl.pallas_call(
        _proj_res_norm_kernel,
        grid=(T // tm,),
        in_specs=[pl.BlockSpec((tm, a.shape[1]), row), pl.BlockSpec(w.shape, fixed),
                  pl.BlockSpec((tm, D), row), pl.BlockSpec((1, D), fixed)],
        out_specs=[pl.BlockSpec((tm, D), row), pl.BlockSpec((tm, D), row)],
        out_shape=[jax.ShapeDtypeStruct((T, D), F32), jax.ShapeDtypeStruct((T, D), BF16)],
        compiler_params=_params(("arbitrary",)),
        name="proj_res_norm",
    )(a, w.astype(BF16), x2d, g[None])


def _ffn_kernel(lo_ref, hi_ref, first_ref, np_ref, h_ref, w1_ref, w3_ref, w2_ref, *rest, residual, single):
    if residual:
        res_ref, o_ref, acc_ref = rest
    else:
        o_ref, acc_ref = rest
    i, c = pl.program_id(0), pl.program_id(1)

    @pl.when(i < np_ref[0])
    def _():
        h = h_ref[...].astype(BF16)
        u = _dot(h, w1_ref[0])
        g = _dot(h, w3_ref[0])
        y = _dot((u * jax.nn.sigmoid(u) * g).astype(BF16), w2_ref[0])
        if single:
            acc_ref[...] = y
        else:
            @pl.when(c == 0)
            def _():
                acc_ref[...] = y

            @pl.when(c > 0)
            def _():
                acc_ref[...] += y

        last = c == pl.num_programs(1) - 1

        def result():
            return res_ref[...] + acc_ref[...] if residual else acc_ref[...]

        @pl.when(last & (first_ref[i] == 1))
        def _():
            o_ref[...] = result()

        @pl.when(last & (first_ref[i] == 0))
        def _():
            rows = lax.broadcasted_iota(jnp.int32, o_ref.shape, 0)
            mine = (rows >= lo_ref[i]) & (rows < hi_ref[i])
            o_ref[...] = jnp.where(mine, result(), o_ref[...])


def _ffn(h, w1, w3, w2, sched, rt, fc, residual=None):
    R, D = h.shape
    E, _, FF = w1.shape
    assert FF % fc == 0 and fc % MXU == 0
    expert, tile, lo, hi, first, count = sched
    hmap = lambda i, c, te, tb, *_: (tb[i], 0)
    mode = dict(pipeline_mode=pl.Buffered(1)) if (FF == fc and E == 1) else {}
    in_specs = [pl.BlockSpec((rt, D), hmap),
                pl.BlockSpec((1, D, fc), lambda i, c, te, *_: (te[i], 0, c), **mode),
                pl.BlockSpec((1, D, fc), lambda i, c, te, *_: (te[i], 0, c), **mode),
                pl.BlockSpec((1, fc, D), lambda i, c, te, *_: (te[i], c, 0), **mode)]
    ins = [h, w1, w3, w2]
    if residual is not None:
        in_specs.append(pl.BlockSpec((rt, D), hmap))
        ins.append(residual)

    def kern(te_ref, tb_ref, *refs):
        _ffn_kernel(*refs, residual=residual is not None, single=FF == fc)

    return pl.pallas_call(
        kern,
        grid_spec=pltpu.PrefetchScalarGridSpec(
            num_scalar_prefetch=6,
            grid=(expert.shape[0], FF // fc),
            in_specs=in_specs,
            out_specs=pl.BlockSpec((rt, D), hmap),
            scratch_shapes=[pltpu.VMEM((rt, D), F32)],
        ),
        out_shape=jax.ShapeDtypeStruct((R, D), F32),
        compiler_params=_params(("arbitrary", "arbitrary")),
        name="ffn",
    )(expert, tile, lo, hi, first, count, *ins)


def _dense_schedule(n_tiles, rt):
    i32 = jnp.int32
    return (jnp.zeros((n_tiles,), i32), jnp.arange(n_tiles, dtype=i32), jnp.zeros((n_tiles,), i32),
            jnp.full((n_tiles,), rt, i32), jnp.ones((n_tiles,), i32), jnp.full((1,), n_tiles, i32))


def _expert_schedule(cnt, n_rows, rt):
    i32 = jnp.int32
    E = cnt.shape[0]
    end = jnp.cumsum(cnt)
    off = end - cnt
    t_first = off // rt
    t_last = jnp.maximum(end - 1, off) // rt
    n_vis = jnp.where(cnt > 0, t_last - t_first + 1, 0)
    vis_end = jnp.cumsum(n_vis)
    count = vis_end[-1]
    max_vis = n_rows // rt + E - 1
    p = jnp.minimum(jnp.arange(max_vis, dtype=i32), jnp.maximum(count - 1, 0))
    expert = jnp.minimum(jnp.sum((p[:, None] >= vis_end[None, :]).astype(i32), axis=1), E - 1)
    tile = t_first[expert] + (p - (vis_end[expert] - n_vis[expert]))
    lo = jnp.clip(off[expert] - tile * rt, 0, rt)
    hi = jnp.clip(end[expert] - tile * rt, 0, rt)
    first = jnp.concatenate([jnp.ones((1,), i32), (tile[1:] != tile[:-1]).astype(i32)])
    return (expert, tile.astype(i32), lo.astype(i32), hi.astype(i32), first, count.reshape(1).astype(i32)), off


def _band_proj_kernel(x_ref, gkv_ref, gq_ref, wk_ref, wv_ref, wq_ref, gkn_ref, gqn_ref, bd_ref,
                      k_ref, v_ref, q_ref):
    x = x_ref[...]
    hk = _rms_rows(x, gkv_ref[...]).astype(BF16)
    hq = _rms_rows(x, gq_ref[...]).astype(BF16)
    v_ref[...] = _dot(hk, wv_ref[...]).astype(BF16)
    bd = bd_ref[...]
    inv_d = 1.0 / CA_DIM

    def project(h, w_ref, g_ref, o_ref, cols):
        return _dot(h, w_ref[:, cols])

    def normalise(h, w_ref, g_ref, o_ref, cols, y):
        ssq = _dot((y * y).astype(BF16), bd)
        o_ref[:, cols] = (y * lax.rsqrt(ssq * inv_d + EPS) * g_ref[...]).astype(BF16)

    units = [(h, w_ref, g_ref, o_ref, slice(p * MXU, (p + 1) * MXU))
             for p in range(k_ref.shape[1] // MXU)
             for h, w_ref, g_ref, o_ref in ((hk, wk_ref, gkn_ref, k_ref), (hq, wq_ref, gqn_ref, q_ref))]
    _run_ahead(units, project, normalise, 2)


def _band_proj(x2d, seq, g_kv, g_attn, w_k, w_v, w_q, g_kn, g_qn):
    T, D = x2d.shape
    HD = CA_HEADS * CA_DIM
    tm = min(512, seq)
    r = np.arange(MXU)
    bd = jnp.asarray((r[:, None] // CA_DIM == r[None, :] // CA_DIM).astype(np.float32), BF16)
    gkn = jnp.tile(g_kn, MXU // CA_DIM)[None]
    gqn = jnp.tile(g_qn, MXU // CA_DIM)[None] * (CA_DIM ** -0.5 * LOG2E)
    fixed = lambda i: (0, 0)
    row = lambda i: (i, 0)
    return pl.pallas_call(
        _band_proj_kernel,
        grid=(T // tm,),
        in_specs=[pl.BlockSpec((tm, D), row), pl.BlockSpec((1, D), fixed), pl.BlockSpec((1, D), fixed),
                  pl.BlockSpec((D, HD), fixed), pl.BlockSpec((D, HD), fixed), pl.BlockSpec((D, HD), fixed),
                  pl.BlockSpec((1, MXU), fixed), pl.BlockSpec((1, MXU), fixed), pl.BlockSpec((MXU, MXU), fixed)],
        out_specs=[pl.BlockSpec((tm, HD), row)] * 3,
        out_shape=[jax.ShapeDtypeStruct((T, HD), BF16)] * 3,
        compiler_params=_params(("arbitrary",)),
        name="band_proj",
    )(x2d, g_kv[None], g_attn[None], w_k.astype(BF16), w_v.astype(BF16), w_q.astype(BF16), gkn, gqn, bd)


def _band_attn_kernel(q_ref, k_ref, v_ref, bias_ref, o_ref, vt_ref, *, tq, pad):
    seq = q_ref.shape[1]
    win = tq + pad
    for j in range(seq // MXU):
        blk = slice(j * MXU, (j + 1) * MXU)
        vt_ref[:, blk] = v_ref[0, blk, :].astype(F32).T.astype(BF16)

    first_head = lax.broadcasted_iota(jnp.int32, (tq, LANES), 1) < CA_DIM

    def key_range(qi):
        return slice(max(0, qi * tq - pad), (qi + 1) * tq)

    def scores(qi):
        keys = key_range(qi)
        q2 = q_ref[0, qi * tq:(qi + 1) * tq, :]
        zero = jnp.zeros_like(q2)
        qbd = jnp.concatenate([jnp.where(first_head, q2, zero), jnp.where(first_head, zero, q2)], axis=0)
        return _dot_nt(k_ref[0, keys, :], qbd) + bias_ref[0, win - (keys.stop - keys.start):, :]

    def finish(qi, s):
        keys = key_range(qi)
        m = jnp.max(s, axis=0, keepdims=True)
        p = jnp.exp2(s - m)
        r = 1.0 / jnp.sum(p, axis=0, keepdims=True)
        pb = p.astype(BF16)
        acc0 = _dot(vt_ref[:CA_DIM, keys], pb[:, :tq]) * r[:, :tq]
        acc1 = _dot(vt_ref[CA_DIM:, keys], pb[:, tq:]) * r[:, tq:]
        o_ref[0, qi * tq:(qi + 1) * tq, :] = jnp.concatenate([acc0, acc1], axis=0).T.astype(BF16)

    _run_ahead([(qi,) for qi in range(seq // tq)], scores, finish, 2)


def _band_bias(rel_bias, tq):
    H = rel_bias.shape[0]
    pad = LEFT_CHUNKS * CHUNK
    win = tq + pad
    n = win + tq - 1
    rel = pad - (np.arange(n) - (tq - 1))
    table = rel_bias[:, np.clip(rel, -REL_CLIP, REL_CLIP) + REL_CLIP].astype(F32) * LOG2E
    period = jnp.concatenate([table, jnp.zeros((H, 1), F32)], axis=1)
    skew = jnp.tile(period, (1, tq))[:, :tq * n].reshape(H, tq, n)
    bias = jnp.transpose(skew[:, :, tq - 1:tq - 1 + win], (0, 2, 1))
    r = np.arange(win)[:, None]
    c = np.arange(tq)[None, :]
    qc = (c + pad) // CHUNK
    kc = r // CHUNK
    vis = (kc <= qc) & (kc >= qc - LEFT_CHUNKS)
    bias = jnp.where(jnp.asarray(vis)[None], bias, NEG)
    return bias.reshape(H // 2, 2, win, tq).transpose(0, 2, 1, 3).reshape(H // 2, win, 2 * tq)


def _band_attn(q, k, v, rel_bias, batch, seq):
    H = CA_HEADS
    HD = H * CA_DIM
    pad = LEFT_CHUNKS * CHUNK
    tq = LANES
    pair = lambda h, b: (b, 0, h)
    shape3 = (batch, seq, HD)
    out = pl.pallas_call(
        functools.partial(_band_attn_kernel, tq=tq, pad=pad),
        grid=(H // 2, batch),
        in_specs=[pl.BlockSpec((1, seq, LANES), pair), pl.BlockSpec((1, seq, LANES), pair),
                  pl.BlockSpec((1, seq, LANES), pair),
                  pl.BlockSpec((1, tq + pad, 2 * tq), lambda h, b: (h, 0, 0))],
        out_specs=pl.BlockSpec((1, seq, LANES), pair),
        out_shape=jax.ShapeDtypeStruct(shape3, BF16),
        scratch_shapes=[pltpu.VMEM((LANES, seq), BF16)],
        compiler_params=_params(("arbitrary", "arbitrary")),
        name="band_attn",
    )(q.reshape(shape3), k.reshape(shape3), v.reshape(shape3), _band_bias(rel_bias, tq))
    return out.reshape(batch * seq, HD)


R_E1, R_E2, R_RANK1, R_RANK2, R_G1, R_G2 = 8, 9, 10, 11, 12, 13


def _router_kernel(a_ref, w_ref, x_ref, g_ref, wrh_ref, wrl_ref, xo_ref, ho_ref, r_ref, cnt_ref, run_ref,
                   *, n_experts):
    i = pl.program_id(0)

    @pl.when(i == 0)
    def _():
        run_ref[...] = jnp.zeros_like(run_ref)

    subs = [slice(j * MXU, (j + 1) * MXU) for j in range(x_ref.shape[0] // MXU)]
    lane = lax.broadcasted_iota(jnp.int32, (MXU, LANES), 1).astype(F32)
    tri = (lax.broadcasted_iota(jnp.int32, (MXU, MXU), 0) > lax.broadcasted_iota(jnp.int32, (MXU, MXU), 1))
    tri = tri.astype(BF16)

    hs = []
    for rows in subs:
        y = x_ref[rows, :] + _dot(a_ref[rows, :], w_ref[...])
        xo_ref[rows, :] = y
        hf = _rms_rows(y, g_ref[...])
        ho_ref[rows, :] = hf
        h_hi = hf.astype(BF16)
        hs.append((h_hi, (hf - h_hi.astype(F32)).astype(BF16)))
    logit_list = [_dot(h_hi, wrh_ref[...]) + (_dot(h_lo, wrh_ref[...]) + _dot(h_hi, wrl_ref[...]))
                  for h_hi, h_lo in hs]
    picks = []
    for logits in logit_list:
        logits = jnp.where(lane < n_experts, logits, NEG)
        m1 = jnp.max(logits, axis=-1, keepdims=True)
        e1 = jnp.min(jnp.where(logits == m1, lane, float(LANES)), axis=-1, keepdims=True)
        rest = jnp.where(lane == e1, NEG, logits)
        m2 = jnp.max(rest, axis=-1, keepdims=True)
        e2 = jnp.min(jnp.where(rest == m2, lane, float(LANES)), axis=-1, keepdims=True)
        ex = jnp.exp(m2 - m1)
        g1 = 1.0 / (1.0 + ex)
        g2 = ex * g1
        oh1 = (lane == e1).astype(F32)
        oh2 = (lane == e2).astype(F32)
        picks.append((e1, e2, g1, g2, oh1, oh2))
    befores = [_dot(tri, (oh1 + oh2).astype(BF16)) for _, _, _, _, oh1, oh2 in picks]
    run = run_ref[...]
    for rows, (e1, e2, g1, g2, oh1, oh2), before in zip(subs, picks, befores):
        before = before + run
        rank1 = jnp.sum(oh1 * before, axis=-1, keepdims=True)
        rank2 = jnp.sum(oh2 * before, axis=-1, keepdims=True)
        run = run + jnp.sum(oh1 + oh2, axis=0, keepdims=True)
        rec = oh1 * g1 + oh2 * g2
        for ln, val in ((R_E1, e1), (R_E2, e2), (R_RANK1, rank1), (R_RANK2, rank2), (R_G1, g1), (R_G2, g2)):
            rec = jnp.where(lane == ln, val, rec)
        r_ref[rows, :] = rec
    run_ref[...] = run
    cnt_ref[...] = run


def _router(a, w, x2d, g, w_router, seq):
    T, D = x2d.shape
    E = w_router.shape[1]
    tm = min(512, seq)
    wr = jnp.zeros((D, LANES), F32).at[:, :E].set(w_router)
    wr_hi = wr.astype(BF16)
    wr_lo = (wr - wr_hi.astype(F32)).astype(BF16)
    row = lambda i: (i, 0)
    fixed = lambda i: (0, 0)
    return pl.pallas_call(
        functools.partial(_router_kernel, n_experts=E),
        grid=(T // tm,),
        in_specs=[pl.BlockSpec((tm, a.shape[1]), row), pl.BlockSpec(w.shape, fixed),
                  pl.BlockSpec((tm, D), row), pl.BlockSpec((1, D), fixed),
                  pl.BlockSpec((D, LANES), fixed), pl.BlockSpec((D, LANES), fixed)],
        out_specs=[pl.BlockSpec((tm, D), row), pl.BlockSpec((tm, D), row),
                   pl.BlockSpec((tm, LANES), row), pl.BlockSpec((1, LANES), fixed)],
        out_shape=[jax.ShapeDtypeStruct((T, D), F32), jax.ShapeDtypeStruct((T, D), F32),
                   jax.ShapeDtypeStruct((T, LANES), F32), jax.ShapeDtypeStruct((1, LANES), F32)],
        scratch_shapes=[pltpu.VMEM((1, LANES), F32)],
        compiler_params=_params(("arbitrary",)),
        name="router",
    )(a, w.astype(BF16), x2d, g[None], wr_hi, wr_lo)


def _scatter_rows_kernel(dest_ref, h_ref, o_ref, idx_ref, isem, sem, *, tm):
    i = pl.program_id(0)
    icopy = pltpu.make_async_copy(dest_ref.at[pl.ds(i * (TOP_K * tm), TOP_K * tm)], idx_ref, isem)
    icopy.start()
    icopy.wait()

    def row_copy(r, d):
        return pltpu.make_async_copy(h_ref.at[pl.ds(r, 1)], o_ref.at[pl.ds(d, 1)], sem)

    def issue(r, _):
        row_copy(r, idx_ref[r]).start(priority=0)
        row_copy(r, idx_ref[tm + r]).start(priority=1)
        return 0

    lax.fori_loop(0, tm, issue, 0, unroll=8)
    for _ in range(TOP_K * tm):
        row_copy(0, 0).wait()


def _scatter_rows(h, dest, rows_out, tm):
    T, D = h.shape
    return pl.pallas_call(
        functools.partial(_scatter_rows_kernel, tm=tm),
        grid=(T // tm,),
        in_specs=[pl.BlockSpec(memory_space=pl.ANY), pl.BlockSpec((tm, D), lambda i: (i, 0))],
        out_specs=pl.BlockSpec(memory_space=pl.ANY),
        out_shape=jax.ShapeDtypeStruct((rows_out, D), h.dtype),
        scratch_shapes=[pltpu.SMEM((TOP_K * tm,), jnp.int32), pltpu.SemaphoreType.DMA, pltpu.SemaphoreType.DMA],
        compiler_params=_params(("arbitrary",)),
        name="moe_scatter",
    )(dest, h)


def _combine_kernel(dest_ref, y_ref, x_ref, r_ref, o_ref, idx_ref, b1_ref, b2_ref, isem, sem, *, tm):
    i = pl.program_id(0)
    icopy = pltpu.make_async_copy(dest_ref.at[pl.ds(i * (TOP_K * tm), TOP_K * tm)], idx_ref, isem)
    icopy.start()
    icopy.wait()

    def row_copy(d, buf, r):
        return pltpu.make_async_copy(y_ref.at[pl.ds(d, 1)], buf.at[pl.ds(r, 1)], sem)

    def issue(r, _):
        row_copy(idx_ref[r], b1_ref, r).start(priority=0)
        row_copy(idx_ref[tm + r], b2_ref, r).start(priority=1)
        return 0

    lax.fori_loop(0, tm, issue, 0, unroll=8)
    for _ in range(tm):
        row_copy(0, b1_ref, 0).wait()
        row_copy(0, b2_ref, 0).wait()
    rec = r_ref[...]
    lane = lax.broadcasted_iota(jnp.int32, rec.shape, 1)
    g1 = jnp.sum(jnp.where(lane == R_G1, rec, 0.0), axis=-1, keepdims=True)
    g2 = jnp.sum(jnp.where(lane == R_G2, rec, 0.0), axis=-1, keepdims=True)
    o_ref[...] = x_ref[...] + (g1 * b1_ref[...] + g2 * b2_ref[...])


def _combine(y_sorted, dest, x2d, rec, tm):
    T, D = x2d.shape
    row = lambda i: (i, 0)
    return pl.pallas_call(
        functools.partial(_combine_kernel, tm=tm),
        grid=(T // tm,),
        in_specs=[pl.BlockSpec(memory_space=pl.ANY), pl.BlockSpec(memory_space=pl.ANY),
                  pl.BlockSpec((tm, D), row), pl.BlockSpec((tm, LANES), row)],
        out_specs=pl.BlockSpec((tm, D), row),
        out_shape=jax.ShapeDtypeStruct((T, D), F32),
        scratch_shapes=[pltpu.SMEM((TOP_K * tm,), jnp.int32), pltpu.VMEM((tm, D), F32), pltpu.VMEM((tm, D), F32),
                        pltpu.SemaphoreType.DMA, pltpu.SemaphoreType.DMA],
        compiler_params=_params(("arbitrary",)),
        name="moe_combine",
    )(dest, y_sorted, x2d, rec)


def _moe_chunk(ff):
    n = ff // MXU
    return max(d for d in range(1, n + 1) if n % d == 0 and d <= 8) * MXU


def _moe(h, x2d, rec, counts, w1, w3, w2, seq):
    T = h.shape[0]
    E = w1.shape[0]
    rt = min(512, T)
    tm = min(512, seq)
    cnt = counts[0, :E].astype(jnp.int32)
    sched, group_start = _expert_schedule(cnt, TOP_K * T, rt)
    e1 = rec[:, R_E1].astype(jnp.int32)
    e2 = rec[:, R_E2].astype(jnp.int32)
    d1 = group_start[e1] + rec[:, R_RANK1].astype(jnp.int32)
    d2 = group_start[e2] + rec[:, R_RANK2].astype(jnp.int32)
    dest = jnp.concatenate([d1.reshape(T // tm, tm), d2.reshape(T // tm, tm)], axis=1).reshape(-1)

    h_sorted = _scatter_rows(h, dest, TOP_K * T, tm)
    y_sorted = _ffn(h_sorted, w1, w3, w2, sched, rt, _moe_chunk(w1.shape[2]))
    return _combine(y_sorted, dest, x2d, rec, tm)


def kernel(x, attn_norm, ffn_norm, a_w_dq, a_g_cq, a_w_uq, a_w_dkv, a_g_ckv, a_w_ukv, a_g_qn, a_g_kn, a_w_o,
           kv_norm, kv_w_k, kv_w_v, kv_g_kn, b_w_q, b_g_qn, b_rel_bias, b_w_o, ffn_w1, ffn_w3, ffn_w2,
           moe_router, moe_w1, moe_w3, moe_w2):
    B, S, D = x.shape
    T = B * S
    assert S % 256 == 0 and T % 512 == 0
    x0 = x.reshape(T, D)

    q, k, v = _mla_proj(x0, S, attn_norm[0], a_w_dq[0], a_g_cq[0], a_w_uq[0], a_w_dkv[0], a_g_ckv[0],
                        a_w_ukv[0], a_g_qn[0], a_g_kn[0])
    o = _mla_attn(q, k, v, B, S)
    x1, h1 = _proj_res_norm(o, a_w_o[0], x0, ffn_norm[0], S)
    rt = min(512, T)
    x2 = _ffn(h1, ffn_w1.astype(BF16), ffn_w3.astype(BF16), ffn_w2.astype(BF16),
              _dense_schedule(T // rt, rt), rt, ffn_w1.shape[2], residual=x1)

    kb, vb, qb = _band_proj(x2, S, kv_norm, attn_norm[1], kv_w_k, kv_w_v, b_w_q[0], kv_g_kn, b_g_qn[0])
    ob = _band_attn(qb, kb, vb, b_rel_bias[0], B, S)
    x3, h3, rec, counts = _router(ob, b_w_o[0], x2, ffn_norm[1], moe_router[0], S)
    x4 = _moe(h3, x3, rec, counts, moe_w1[0].astype(BF16), moe_w3[0].astype(BF16), moe_w2[0].astype(BF16), S)
    return x4.reshape(B, S, D)
```

```python
import functools

import jax
import jax.numpy as jnp
import numpy as np
from jax import lax
from jax.experimental import pallas as pl
from jax.experimental.pallas import tpu as pltpu

F32 = jnp.float32
BF16 = jnp.bfloat16

EPS = 1e-6
NEG = -1e30
ROPE_THETA = 10000.0
LOG2E = 1.4426950408889634

LANES = 128
MXU = 256
VMEM_LIMIT = 56 * 1024 * 1024

CHUNK = 64
LEFT_CHUNKS = 8
REL_CLIP = 128
MLA_HEADS = 16
MLA_NOPE = 64
MLA_ROPE = 32
MLA_QK = MLA_NOPE + MLA_ROPE
MLA_V = 64
CA_HEADS = 16
CA_DIM = 64
TOP_K = 2


def _params(sem):
    return pltpu.CompilerParams(dimension_semantics=sem, vmem_limit_bytes=VMEM_LIMIT)


def _rms_rows(x, g):
    ms = jnp.mean(x * x, axis=-1, keepdims=True)
    return x * lax.rsqrt(ms + EPS) * g


def _dot(a, b):
    return jnp.dot(a, b, preferred_element_type=F32)


def _dot_nt(a, b):
    return lax.dot_general(a, b, (((1,), (1,)), ((), ())), preferred_element_type=F32)


def _run_ahead(units, produce, consume, ahead):
    pending = [produce(*u) for u in units[:ahead]]
    for n, u in enumerate(units):
        if n + ahead < len(units):
            pending.append(produce(*units[n + ahead]))
        consume(*u, pending.pop(0))


def _mla_proj_kernel(x_ref, gn_ref, wdq_ref, gcq_ref, wuq_ref, wdkv_ref, gckv_ref, wuk_ref, wuv_ref,
                     gq_ref, gkn_ref, ga_ref, gb_ref, tabq_ref, cosk_ref, sink_ref, bd_ref,
                     q_ref, k_ref, v_ref):
    kv_lora = gckv_ref.shape[1]
    h = _rms_rows(x_ref[...], gn_ref[...]).astype(BF16)
    cq = _rms_rows(_dot(h, wdq_ref[...]), gcq_ref[...]).astype(BF16)
    kva = _dot(h, wdkv_ref[...])
    ckv = _rms_rows(kva[:, :kv_lora], gckv_ref[...]).astype(BF16)
    slab_a = kva[:, kv_lora:kv_lora + LANES]
    slab_b = kva[:, kv_lora + LANES:kv_lora + 2 * LANES]
    lane = lax.broadcasted_iota(jnp.int32, slab_a.shape, 1)
    rope_lanes = (lane >= MLA_NOPE) & (lane < MLA_QK)
    kpe_ssq = jnp.sum(jnp.where(rope_lanes, slab_a * slab_a, 0.0), axis=-1, keepdims=True)
    k_rot = slab_a * (ga_ref[...] * cosk_ref[...]) + slab_b * (gb_ref[...] * sink_ref[...])
    v_ref[...] = _dot(ckv, wuv_ref[...]).astype(BF16)

    gq = gq_ref[...] * tabq_ref[...]
    gq2 = jnp.concatenate([gq, gq], axis=1)
    k_rot2 = jnp.concatenate([k_rot, k_rot], axis=1)
    gkn2 = jnp.concatenate([gkn_ref[...], gkn_ref[...]], axis=1)
    bd = bd_ref[...]
    inv_d = 1.0 / MLA_QK

    def project(kind, cols):
        return _dot(cq, wuq_ref[:, cols]) if kind == "q" else _dot(ckv, wuk_ref[:, cols])

    def normalise(kind, cols, y):
        ssq = _dot((y * y).astype(BF16), bd)
        if kind == "q":
            q_ref[:, cols] = (y * lax.rsqrt(ssq * inv_d + EPS) * gq2).astype(BF16)
        else:
            k_ref[:, cols] = ((y * gkn2 + k_rot2) * lax.rsqrt((ssq + kpe_ssq) * inv_d + EPS)).astype(BF16)

    units = [(kind, slice(p * MXU, (p + 1) * MXU)) for p in range(q_ref.shape[1] // MXU) for kind in ("q", "k")]
    _run_ahead(units, project, normalise, 2)


def _mla_proj(x2d, seq, g_attn, w_dq, g_cq, w_uq, w_dkv, g_ckv, w_ukv, g_qn, g_kn):
    T, D = x2d.shape
    H, NOPE, ROPE, QK, VD = MLA_HEADS, MLA_NOPE, MLA_ROPE, MLA_QK, MLA_V
    half = ROPE // 2
    q_lora = w_dq.shape[1]
    kv_lora = g_ckv.shape[0]
    tm = min(512, seq)

    wq = w_uq.reshape(q_lora, H, QK)
    wq_n, wq_1, wq_2 = wq[..., :NOPE], wq[..., NOPE:NOPE + half], wq[..., NOPE + half:]
    wuq_p = jnp.concatenate([wq_n, wq_1, wq_2, wq_2, wq_1], axis=-1).reshape(q_lora, H * LANES).astype(BF16)
    wkv = w_ukv.reshape(kv_lora, H, NOPE + VD)
    wuk_p = jnp.concatenate([wkv[..., :NOPE], jnp.zeros((kv_lora, H, LANES - NOPE), F32)], axis=-1)
    wuk_p = wuk_p.reshape(kv_lora, H * LANES).astype(BF16)
    wuv_p = wkv[..., NOPE:].reshape(kv_lora, H * VD).astype(BF16)
    pe1, pe2 = w_dkv[:, kv_lora:kv_lora + half], w_dkv[:, kv_lora + half:]
    z = jnp.zeros((D, NOPE), F32)
    wdkv_p = jnp.concatenate([w_dkv[:, :kv_lora], z, pe1, pe2, pe1, pe2, z, pe2, pe1, pe2, pe1], axis=1).astype(BF16)

    gq = jnp.concatenate([g_qn[:NOPE], g_qn[NOPE:NOPE + half], g_qn[NOPE + half:],
                          g_qn[NOPE + half:], g_qn[NOPE:NOPE + half]])[None]
    k1, k2 = g_kn[NOPE:NOPE + half], g_kn[NOPE + half:]
    zn = jnp.zeros((NOPE,), F32)
    gkn = jnp.concatenate([g_kn[:NOPE], zn])[None]
    ga = jnp.concatenate([zn, k1, k2, k1, k2])[None]
    gb = jnp.concatenate([zn, k2, k1, k2, k1])[None]

    inv = ROPE_THETA ** (-jnp.arange(half, dtype=F32) / half)
    ang = jnp.arange(seq, dtype=F32)[:, None] * inv[None, :]
    c, s = jnp.cos(ang), jnp.sin(ang)
    ones = jnp.ones((seq, NOPE), F32)
    zs = jnp.zeros((seq, NOPE), F32)
    tabq = jnp.concatenate([ones, c, c, -s, s], axis=1) * (QK ** -0.5 * LOG2E)
    cosk = jnp.concatenate([zs, c, c, c, c], axis=1)
    sink = jnp.concatenate([zs, -s, s, -s, s], axis=1)

    r = np.arange(MXU)
    bd = ((r[:, None] // LANES == r[None, :] // LANES) & (r[:, None] % LANES < QK)).astype(np.float32)
    bd = jnp.asarray(bd, BF16)

    nseq = seq // tm
    row = lambda i: (i, 0)
    fixed = lambda i: (0, 0)
    pos = lambda i: (i % nseq, 0)
    full = lambda a: pl.BlockSpec(a.shape, fixed)
    ins = [x2d, g_attn[None], w_dq.astype(BF16), g_cq[None], wuq_p, wdkv_p, g_ckv[None], wuk_p, wuv_p,
           gq, gkn, ga, gb, tabq, cosk, sink, bd]
    in_specs = [pl.BlockSpec((tm, D), row)] + [full(a) for a in ins[1:13]]
    in_specs += [pl.BlockSpec((tm, LANES), pos)] * 3 + [full(bd)]
    return pl.pallas_call(
        _mla_proj_kernel,
        grid=(T // tm,),
        in_specs=in_specs,
        out_specs=[pl.BlockSpec((tm, H * LANES), row), pl.BlockSpec((tm, H * LANES), row),
                   pl.BlockSpec((tm, H * VD), row)],
        out_shape=[jax.ShapeDtypeStruct((T, H * LANES), BF16), jax.ShapeDtypeStruct((T, H * LANES), BF16),
                   jax.ShapeDtypeStruct((T, H * VD), BF16)],
        compiler_params=_params(("arbitrary",)),
        name="mla_proj",
    )(*ins)


def _mla_attn_kernel(q_ref, k_ref, v_ref, o_ref, vt_ref, *, tq):
    seq = q_ref.shape[1]
    nq = seq // tq
    for j in range(nq):
        blk = slice(j * tq, (j + 1) * tq)
        vt_ref[:, blk] = v_ref[0, blk, :].astype(F32).T.astype(BF16)

    krow = lax.broadcasted_iota(jnp.int32, (tq, tq), 0) // CHUNK
    qcol = lax.broadcasted_iota(jnp.int32, (tq, tq), 1) // CHUNK
    diag_ok = krow <= qcol

    def scores(qi, hh):
        rows = slice(qi * tq, (qi + 1) * tq)
        hs = slice(hh * LANES, (hh + 1) * LANES)
        q = q_ref[0, rows, hs]
        sd = jnp.where(diag_ok, _dot_nt(k_ref[0, rows, hs], q), NEG)
        so = _dot_nt(k_ref[0, :qi * tq, hs], q) if qi else None
        return sd, so

    def finish(qi, hh, s):
        sd, so = s
        rows = slice(qi * tq, (qi + 1) * tq)
        past = slice(0, qi * tq)
        vs = slice(hh * MLA_V, (hh + 1) * MLA_V)
        m = jnp.max(sd, axis=0, keepdims=True)
        if qi:
            m = jnp.maximum(m, jnp.max(so, axis=0, keepdims=True))
        pd = jnp.exp2(sd - m)
        l = jnp.sum(pd, axis=0, keepdims=True)
        acc = _dot(vt_ref[vs, rows], pd.astype(BF16))
        if qi:
            po = jnp.exp2(so - m)
            l = l + jnp.sum(po, axis=0, keepdims=True)
            acc = acc + _dot(vt_ref[vs, past], po.astype(BF16))
        o_ref[0, rows, vs] = (acc * (1.0 / l)).T.astype(BF16)

    _run_ahead([(qi, hh) for qi in range(nq) for hh in range(2)], scores, finish, 4)


def _mla_attn(q, k, v, batch, seq):
    H = MLA_HEADS
    q3 = q.reshape(batch, seq, H * LANES)
    k3 = k.reshape(batch, seq, H * LANES)
    v3 = v.reshape(batch, seq, H * MLA_V)
    pair = lambda b, h: (b, 0, h)
    out = pl.pallas_call(
        functools.partial(_mla_attn_kernel, tq=min(256, seq)),
        grid=(batch, H // 2),
        in_specs=[pl.BlockSpec((1, seq, 2 * LANES), pair), pl.BlockSpec((1, seq, 2 * LANES), pair),
                  pl.BlockSpec((1, seq, 2 * MLA_V), pair)],
        out_specs=pl.BlockSpec((1, seq, 2 * MLA_V), pair),
        out_shape=jax.ShapeDtypeStruct((batch, seq, H * MLA_V), BF16),
        scratch_shapes=[pltpu.VMEM((2 * MLA_V, seq), BF16)],
        compiler_params=_params(("arbitrary", "arbitrary")),
        name="mla_attn",
    )(q3, k3, v3)
    return out.reshape(batch * seq, H * MLA_V)


def _dense_block_kernel(a_ref, wo_ref, x_ref, g_ref, w1_ref, w3_ref, w2_ref, o_ref):
    x1 = x_ref[...] + _dot(a_ref[...], wo_ref[...])
    h = _rms_rows(x1, g_ref[...]).astype(BF16)
    u = _dot(h, w1_ref[...])
    g = _dot(h, w3_ref[...])
    o_ref[...] = x1 + _dot((u * jax.nn.sigmoid(u) * g).astype(BF16), w2_ref[...])


def _dense_block(a, w_o, x2d, g, w1, w3, w2, seq):
    T, D = x2d.shape
    tm = min(512, seq)
    row = lambda i: (i, 0)
    resident = lambda w: pl.BlockSpec(w.shape, lambda i: (0, 0), pipeline_mode=pl.Buffered(1))
    return pl.pallas_call(
        _dense_block_kernel,
        grid=(T // tm,),
        in_specs=[pl.BlockSpec((tm, a.shape[1]), row), resident(w_o), pl.BlockSpec((tm, D), row),
                  pl.BlockSpec((1, D), lambda i: (0, 0)), resident(w1), resident(w3), resident(w2)],
        out_specs=pl.BlockSpec((tm, D), row),
        out_shape=jax.ShapeDtypeStruct((T, D), F32),
        compiler_params=_params(("arbitrary",)),
        name="dense_block",
    )(a, w_o.astype(BF16), x2d, g[None], w1.astype(BF16), w3.astype(BF16), w2.astype(BF16))


def _ffn_kernel(lo_ref, hi_ref, first_ref, np_ref, h_ref, w1_ref, w3_ref, w2_ref, o_ref, acc_ref, *, single):
    i, c = pl.program_id(0), pl.program_id(1)

    @pl.when(i < np_ref[0])
    def _():
        h = h_ref[...].astype(BF16)
        u = _dot(h, w1_ref[0])
        g = _dot(h, w3_ref[0])
        y = _dot((u * jax.nn.sigmoid(u) * g).astype(BF16), w2_ref[0])
        if single:
            acc_ref[...] = y
        else:
            @pl.when(c == 0)
            def _():
                acc_ref[...] = y

            @pl.when(c > 0)
            def _():
                acc_ref[...] += y

        last = c == pl.num_programs(1) - 1

        @pl.when(last & (first_ref[i] == 1))
        def _():
            o_ref[...] = acc_ref[...]

        @pl.when(last & (first_ref[i] == 0))
        def _():
            rows = lax.broadcasted_iota(jnp.int32, o_ref.shape, 0)
            mine = (rows >= lo_ref[i]) & (rows < hi_ref[i])
            o_ref[...] = jnp.where(mine, acc_ref[...], o_ref[...])


def _ffn(h, w1, w3, w2, sched, rt, fc):
    R, D = h.shape
    FF = w1.shape[2]
    assert FF % fc == 0 and fc % MXU == 0
    expert, tile, lo, hi, first, count = sched
    hmap = lambda i, c, te, tb, *_: (tb[i], 0)
    mode = dict(pipeline_mode=pl.Buffered(1)) if FF == fc else {}
    in_specs = [pl.BlockSpec((rt, D), hmap),
                pl.BlockSpec((1, D, fc), lambda i, c, te, *_: (te[i], 0, c), **mode),
                pl.BlockSpec((1, D, fc), lambda i, c, te, *_: (te[i], 0, c), **mode),
                pl.BlockSpec((1, fc, D), lambda i, c, te, *_: (te[i], c, 0), **mode)]

    def kern(te_ref, tb_ref, *refs):
        _ffn_kernel(*refs, single=FF == fc)

    return pl.pallas_call(
        kern,
        grid_spec=pltpu.PrefetchScalarGridSpec(
            num_scalar_prefetch=6,
            grid=(expert.shape[0], FF // fc),
            in_specs=in_specs,
            out_specs=pl.BlockSpec((rt, D), hmap),
            scratch_shapes=[pltpu.VMEM((rt, D), F32)],
        ),
        out_shape=jax.ShapeDtypeStruct((R, D), F32),
        compiler_params=_params(("arbitrary", "arbitrary")),
        name="ffn",
    )(expert, tile, lo, hi, first, count, h, w1, w3, w2)


def _expert_schedule(cnt, n_rows, rt):
    i32 = jnp.int32
    E = cnt.shape[0]
    end = jnp.cumsum(cnt)
    off = end - cnt
    t_first = off // rt
    t_last = jnp.maximum(end - 1, off) // rt
    n_vis = jnp.where(cnt > 0, t_last - t_first + 1, 0)
    vis_end = jnp.cumsum(n_vis)
    count = vis_end[-1]
    max_vis = n_rows // rt + E - 1
    p = jnp.minimum(jnp.arange(max_vis, dtype=i32), jnp.maximum(count - 1, 0))
    expert = jnp.minimum(jnp.sum((p[:, None] >= vis_end[None, :]).astype(i32), axis=1), E - 1)
    tile = t_first[expert] + (p - (vis_end[expert] - n_vis[expert]))
    lo = jnp.clip(off[expert] - tile * rt, 0, rt)
    hi = jnp.clip(end[expert] - tile * rt, 0, rt)
    first = jnp.concatenate([jnp.ones((1,), i32), (tile[1:] != tile[:-1]).astype(i32)])
    return (expert, tile.astype(i32), lo.astype(i32), hi.astype(i32), first, count.reshape(1).astype(i32)), off


def _band_proj_kernel(x_ref, gkv_ref, gq_ref, wk_ref, wv_ref, wq_ref, gkn_ref, gqn_ref, bd_ref,
                      k_ref, v_ref, q_ref):
    x = x_ref[...]
    hk = _rms_rows(x, gkv_ref[...]).astype(BF16)
    hq = _rms_rows(x, gq_ref[...]).astype(BF16)
    v_ref[...] = _dot(hk, wv_ref[...]).astype(BF16)
    bd = bd_ref[...]
    inv_d = 1.0 / CA_DIM

    def project(h, w_ref, g_ref, o_ref, cols):
        return _dot(h, w_ref[:, cols])

    def normalise(h, w_ref, g_ref, o_ref, cols, y):
        ssq = _dot((y * y).astype(BF16), bd)
        o_ref[:, cols] = (y * lax.rsqrt(ssq * inv_d + EPS) * g_ref[...]).astype(BF16)

    units = [(h, w_ref, g_ref, o_ref, slice(p * MXU, (p + 1) * MXU))
             for p in range(k_ref.shape[1] // MXU)
             for h, w_ref, g_ref, o_ref in ((hk, wk_ref, gkn_ref, k_ref), (hq, wq_ref, gqn_ref, q_ref))]
    _run_ahead(units, project, normalise, 2)


def _band_proj(x2d, seq, g_kv, g_attn, w_k, w_v, w_q, g_kn, g_qn):
    T, D = x2d.shape
    HD = CA_HEADS * CA_DIM
    tm = min(512, seq)
    r = np.arange(MXU)
    bd = jnp.asarray((r[:, None] // CA_DIM == r[None, :] // CA_DIM).astype(np.float32), BF16)
    gkn = jnp.tile(g_kn, MXU // CA_DIM)[None]
    gqn = jnp.tile(g_qn, MXU // CA_DIM)[None] * (CA_DIM ** -0.5 * LOG2E)
    fixed = lambda i: (0, 0)
    row = lambda i: (i, 0)
    return pl.pallas_call(
        _band_proj_kernel,
        grid=(T // tm,),
        in_specs=[pl.BlockSpec((tm, D), row), pl.BlockSpec((1, D), fixed), pl.BlockSpec((1, D), fixed),
                  pl.BlockSpec((D, HD), fixed), pl.BlockSpec((D, HD), fixed), pl.BlockSpec((D, HD), fixed),
                  pl.BlockSpec((1, MXU), fixed), pl.BlockSpec((1, MXU), fixed), pl.BlockSpec((MXU, MXU), fixed)],
        out_specs=[pl.BlockSpec((tm, HD), row)] * 3,
        out_shape=[jax.ShapeDtypeStruct((T, HD), BF16)] * 3,
        compiler_params=_params(("arbitrary",)),
        name="band_proj",
    )(x2d, g_kv[None], g_attn[None], w_k.astype(BF16), w_v.astype(BF16), w_q.astype(BF16), gkn, gqn, bd)


def _band_attn_kernel(q_ref, k_ref, v_ref, bias_ref, o_ref, vt_ref, *, tq, pad):
    seq = q_ref.shape[1]
    win = tq + pad
    for j in range(seq // MXU):
        blk = slice(j * MXU, (j + 1) * MXU)
        vt_ref[:, blk] = v_ref[0, blk, :].astype(F32).T.astype(BF16)

    first_head = lax.broadcasted_iota(jnp.int32, (tq, LANES), 1) < CA_DIM

    def key_range(qi):
        return slice(max(0, qi * tq - pad), (qi + 1) * tq)

    def scores(qi):
        keys = key_range(qi)
        q2 = q_ref[0, qi * tq:(qi + 1) * tq, :]
        zero = jnp.zeros_like(q2)
        qbd = jnp.concatenate([jnp.where(first_head, q2, zero), jnp.where(first_head, zero, q2)], axis=0)
        return _dot_nt(k_ref[0, keys, :], qbd) + bias_ref[0, win - (keys.stop - keys.start):, :]

    def finish(qi, s):
        keys = key_range(qi)
        m = jnp.max(s, axis=0, keepdims=True)
        p = jnp.exp2(s - m)
        r = 1.0 / jnp.sum(p, axis=0, keepdims=True)
        pb = p.astype(BF16)
        acc0 = _dot(vt_ref[:CA_DIM, keys], pb[:, :tq]) * r[:, :tq]
        acc1 = _dot(vt_ref[CA_DIM:, keys], pb[:, tq:]) * r[:, tq:]
        o_ref[0, qi * tq:(qi + 1) * tq, :] = jnp.concatenate([acc0, acc1], axis=0).T.astype(BF16)

    _run_ahead([(qi,) for qi in range(seq // tq)], scores, finish, 2)


def _band_bias(rel_bias, tq):
    H = rel_bias.shape[0]
    pad = LEFT_CHUNKS * CHUNK
    win = tq + pad
    n = win + tq - 1
    rel = pad - (np.arange(n) - (tq - 1))
    table = rel_bias[:, np.clip(rel, -REL_CLIP, REL_CLIP) + REL_CLIP].astype(F32) * LOG2E
    period = jnp.concatenate([table, jnp.zeros((H, 1), F32)], axis=1)
    skew = jnp.tile(period, (1, tq))[:, :tq * n].reshape(H, tq, n)
    bias = jnp.transpose(skew[:, :, tq - 1:tq - 1 + win], (0, 2, 1))
    r = np.arange(win)[:, None]
    c = np.arange(tq)[None, :]
    qc = (c + pad) // CHUNK
    kc = r // CHUNK
    vis = (kc <= qc) & (kc >= qc - LEFT_CHUNKS)
    bias = jnp.where(jnp.asarray(vis)[None], bias, NEG)
    return bias.reshape(H // 2, 2, win, tq).transpose(0, 2, 1, 3).reshape(H // 2, win, 2 * tq)


def _band_attn(q, k, v, rel_bias, batch, seq):
    H = CA_HEADS
    HD = H * CA_DIM
    pad = LEFT_CHUNKS * CHUNK
    tq = LANES
    pair = lambda h, b: (b, 0, h)
    shape3 = (batch, seq, HD)
    out = pl.pallas_call(
        functools.partial(_band_attn_kernel, tq=tq, pad=pad),
        grid=(H // 2, batch),
        in_specs=[pl.BlockSpec((1, seq, LANES), pair), pl.BlockSpec((1, seq, LANES), pair),
                  pl.BlockSpec((1, seq, LANES), pair),
                  pl.BlockSpec((1, tq + pad, 2 * tq), lambda h, b: (h, 0, 0))],
        out_specs=pl.BlockSpec((1, seq, LANES), pair),
        out_shape=jax.ShapeDtypeStruct(shape3, BF16),
        scratch_shapes=[pltpu.VMEM((LANES, seq), BF16)],
        compiler_params=_params(("arbitrary", "arbitrary")),
        name="band_attn",
    )(q.reshape(shape3), k.reshape(shape3), v.reshape(shape3), _band_bias(rel_bias, tq))
    return out.reshape(batch * seq, HD)


R_E1, R_E2, R_RANK1, R_RANK2, R_G1, R_G2 = 8, 9, 10, 11, 12, 13


def _router_kernel(a_ref, w_ref, x_ref, g_ref, wrh_ref, wrl_ref, xo_ref, ho_ref, r_ref, cnt_ref, run_ref,
                   *, n_experts):
    i = pl.program_id(0)

    @pl.when(i == 0)
    def _():
        run_ref[...] = jnp.zeros_like(run_ref)

    subs = [slice(j * MXU, (j + 1) * MXU) for j in range(x_ref.shape[0] // MXU)]
    lane = lax.broadcasted_iota(jnp.int32, (MXU, LANES), 1).astype(F32)
    tri = (lax.broadcasted_iota(jnp.int32, (MXU, MXU), 0) > lax.broadcasted_iota(jnp.int32, (MXU, MXU), 1))
    tri = tri.astype(BF16)

    hs = []
    for rows in subs:
        y = x_ref[rows, :] + _dot(a_ref[rows, :], w_ref[...])
        xo_ref[rows, :] = y
        hf = _rms_rows(y, g_ref[...])
        ho_ref[rows, :] = hf
        h_hi = hf.astype(BF16)
        hs.append((h_hi, (hf - h_hi.astype(F32)).astype(BF16)))
    logit_list = [_dot(h_hi, wrh_ref[...]) + (_dot(h_lo, wrh_ref[...]) + _dot(h_hi, wrl_ref[...]))
                  for h_hi, h_lo in hs]
    picks = []
    for logits in logit_list:
        logits = jnp.where(lane < n_experts, logits, NEG)
        m1 = jnp.max(logits, axis=-1, keepdims=True)
        e1 = jnp.min(jnp.where(logits == m1, lane, float(LANES)), axis=-1, keepdims=True)
        rest = jnp.where(lane == e1, NEG, logits)
        m2 = jnp.max(rest, axis=-1, keepdims=True)
        e2 = jnp.min(jnp.where(rest == m2, lane, float(LANES)), axis=-1, keepdims=True)
        ex = jnp.exp(m2 - m1)
        g1 = 1.0 / (1.0 + ex)
        g2 = ex * g1
        oh1 = (lane == e1).astype(F32)
        oh2 = (lane == e2).astype(F32)
        picks.append((e1, e2, g1, g2, oh1, oh2))
    befores = [_dot(tri, (oh1 + oh2).astype(BF16)) for _, _, _, _, oh1, oh2 in picks]
    run = run_ref[...]
    for rows, (e1, e2, g1, g2, oh1, oh2), before in zip(subs, picks, befores):
        before = before + run
        rank1 = jnp.sum(oh1 * before, axis=-1, keepdims=True)
        rank2 = jnp.sum(oh2 * before, axis=-1, keepdims=True)
        run = run + jnp.sum(oh1 + oh2, axis=0, keepdims=True)
        rec = oh1 * g1 + oh2 * g2
        for ln, val in ((R_E1, e1), (R_E2, e2), (R_RANK1, rank1), (R_RANK2, rank2), (R_G1, g1), (R_G2, g2)):
            rec = jnp.where(lane == ln, val, rec)
        r_ref[rows, :] = rec
    run_ref[...] = run
    cnt_ref[...] = run


def _router(a, w, x2d, g, w_router, seq):
    T, D = x2d.shape
    E = w_router.shape[1]
    tm = min(512, seq)
    wr = jnp.zeros((D, LANES), F32).at[:, :E].set(w_router)
    wr_hi = wr.astype(BF16)
    wr_lo = (wr - wr_hi.astype(F32)).astype(BF16)
    row = lambda i: (i, 0)
    fixed = lambda i: (0, 0)
    return pl.pallas_call(
        functools.partial(_router_kernel, n_experts=E),
        grid=(T // tm,),
        in_specs=[pl.BlockSpec((tm, a.shape[1]), row), pl.BlockSpec(w.shape, fixed),
                  pl.BlockSpec((tm, D), row), pl.BlockSpec((1, D), fixed),
                  pl.BlockSpec((D, LANES), fixed), pl.BlockSpec((D, LANES), fixed)],
        out_specs=[pl.BlockSpec((tm, D), row), pl.BlockSpec((tm, D), row),
                   pl.BlockSpec((tm, LANES), row), pl.BlockSpec((1, LANES), fixed)],
        out_shape=[jax.ShapeDtypeStruct((T, D), F32), jax.ShapeDtypeStruct((T, D), F32),
                   jax.ShapeDtypeStruct((T, LANES), F32), jax.ShapeDtypeStruct((1, LANES), F32)],
        scratch_shapes=[pltpu.VMEM((1, LANES), F32)],
        compiler_params=_params(("arbitrary",)),
        name="router",
    )(a, w.astype(BF16), x2d, g[None], wr_hi, wr_lo)


def _scatter_rows_kernel(dest_ref, h_ref, o_ref, idx_ref, isem, sem, *, tm):
    i = pl.program_id(0)
    icopy = pltpu.make_async_copy(dest_ref.at[pl.ds(i * (TOP_K * tm), TOP_K * tm)], idx_ref, isem)
    icopy.start()
    icopy.wait()

    def row_copy(r, d):
        return pltpu.make_async_copy(h_ref.at[pl.ds(r, 1)], o_ref.at[pl.ds(d, 1)], sem)

    def issue(r, _):
        row_copy(r, idx_ref[r]).start(priority=0)
        row_copy(r, idx_ref[tm + r]).start(priority=1)
        return 0

    lax.fori_loop(0, tm, issue, 0, unroll=8)
    for _ in range(TOP_K * tm):
        row_copy(0, 0).wait()


def _scatter_rows(h, dest, rows_out, tm):
    T, D = h.shape
    return pl.pallas_call(
        functools.partial(_scatter_rows_kernel, tm=tm),
        grid=(T // tm,),
        in_specs=[pl.BlockSpec(memory_space=pl.ANY), pl.BlockSpec((tm, D), lambda i: (i, 0))],
        out_specs=pl.BlockSpec(memory_space=pl.ANY),
        out_shape=jax.ShapeDtypeStruct((rows_out, D), h.dtype),
        scratch_shapes=[pltpu.SMEM((TOP_K * tm,), jnp.int32), pltpu.SemaphoreType.DMA, pltpu.SemaphoreType.DMA],
        compiler_params=_params(("arbitrary",)),
        name="moe_scatter",
    )(dest, h)


def _combine_kernel(dest_ref, y_ref, x_ref, r_ref, o_ref, idx_ref, b1_ref, b2_ref, isem, sem, *, tm):
    i = pl.program_id(0)
    icopy = pltpu.make_async_copy(dest_ref.at[pl.ds(i * (TOP_K * tm), TOP_K * tm)], idx_ref, isem)
    icopy.start()
    icopy.wait()

    def row_copy(d, buf, r):
        return pltpu.make_async_copy(y_ref.at[pl.ds(d, 1)], buf.at[pl.ds(r, 1)], sem)

    def issue(r, _):
        row_copy(idx_ref[r], b1_ref, r).start(priority=0)
        row_copy(idx_ref[tm + r], b2_ref, r).start(priority=1)
        return 0

    lax.fori_loop(0, tm, issue, 0, unroll=8)
    for _ in range(tm):
        row_copy(0, b1_ref, 0).wait()
        row_copy(0, b2_ref, 0).wait()
    rec = r_ref[...]
    lane = lax.broadcasted_iota(jnp.int32, rec.shape, 1)
    g1 = jnp.sum(jnp.where(lane == R_G1, rec, 0.0), axis=-1, keepdims=True)
    g2 = jnp.sum(jnp.where(lane == R_G2, rec, 0.0), axis=-1, keepdims=True)
    o_ref[...] = x_ref[...] + (g1 * b1_ref[...] + g2 * b2_ref[...])


def _combine(y_sorted, dest, x2d, rec, tm):
    T, D = x2d.shape
    row = lambda i: (i, 0)
    return pl.pallas_call(
        functools.partial(_combine_kernel, tm=tm),
        grid=(T // tm,),
        in_specs=[pl.BlockSpec(memory_space=pl.ANY), pl.BlockSpec(memory_space=pl.ANY),
                  pl.BlockSpec((tm, D), row), pl.BlockSpec((tm, LANES), row)],
        out_specs=pl.BlockSpec((tm, D), row),
        out_shape=jax.ShapeDtypeStruct((T, D), F32),
        scratch_shapes=[pltpu.SMEM((TOP_K * tm,), jnp.int32), pltpu.VMEM((tm, D), F32), pltpu.VMEM((tm, D), F32),
                        pltpu.SemaphoreType.DMA, pltpu.SemaphoreType.DMA],
        compiler_params=_params(("arbitrary",)),
        name="moe_combine",
    )(dest, y_sorted, x2d, rec)


def _moe(h, x2d, rec, counts, w1, w3, w2, seq):
    T = h.shape[0]
    E = w1.shape[0]
    rt = min(512, T)
    tm = min(512, seq)
    cnt = counts[0, :E].astype(jnp.int32)
    sched, group_start = _expert_schedule(cnt, TOP_K * T, rt)
    e1 = rec[:, R_E1].astype(jnp.int32)
    e2 = rec[:, R_E2].astype(jnp.int32)
    d1 = group_start[e1] + rec[:, R_RANK1].astype(jnp.int32)
    d2 = group_start[e2] + rec[:, R_RANK2].astype(jnp.int32)
    dest = jnp.concatenate([d1.reshape(T // tm, tm), d2.reshape(T // tm, tm)], axis=1).reshape(-1)

    h_sorted = _scatter_rows(h, dest, TOP_K * T, tm)
    y_sorted = _ffn(h_sorted, w1, w3, w2, sched, rt, w1.shape[2])
    return _combine(y_sorted, dest, x2d, rec, tm)


def kernel(x, attn_norm, ffn_norm, a_w_dq, a_g_cq, a_w_uq, a_w_dkv, a_g_ckv, a_w_ukv, a_g_qn, a_g_kn, a_w_o,
           kv_norm, kv_w_k, kv_w_v, kv_g_kn, b_w_q, b_g_qn, b_rel_bias, b_w_o, ffn_w1, ffn_w3, ffn_w2,
           moe_router, moe_w1, moe_w3, moe_w2):
    B, S, D = x.shape
    T = B * S
    assert S % 256 == 0 and T % 512 == 0
    x0 = x.reshape(T, D)

    q, k, v = _mla_proj(x0, S, attn_norm[0], a_w_dq[0], a_g_cq[0], a_w_uq[0], a_w_dkv[0], a_g_ckv[0],
                        a_w_ukv[0], a_g_qn[0], a_g_kn[0])
    o = _mla_attn(q, k, v, B, S)
    x2 = _dense_block(o, a_w_o[0], x0, ffn_norm[0], ffn_w1[0], ffn_w3[0], ffn_w2[0], S)

    kb, vb, qb = _band_proj(x2, S, kv_norm, attn_norm[1], kv_w_k, kv_w_v, b_w_q[0], kv_g_kn, b_g_qn[0])
    ob = _band_attn(qb, kb, vb, b_rel_bias[0], B, S)
    x3, h3, rec, counts = _router(ob, b_w_o[0], x2, ffn_norm[1], moe_router[0], S)
    x4 = _moe(h3, x3, rec, counts, moe_w1[0].astype(BF16), moe_w3[0].astype(BF16), moe_w2[0].astype(BF16), S)
    return x4.reshape(B, S, D)
```
